```python
import math
import jax, jax.numpy as jnp
from jax import lax
import numpy as np

D_MODEL = 1024
BATCH = 2
SEQ = 8192
DEPTH = 1

CHUNK = 64
Q_BLOCK = 128
EPS = 1e-6

ATT_HEADS = 8
ATT_HEAD_DIM = D_MODEL // ATT_HEADS // 2
ATT_V_DIM = 2 * ATT_HEAD_DIM
ATT_WIDTH = ATT_HEADS * ATT_V_DIM

SSD_EXPAND = 2
SSD_INNER = SSD_EXPAND * D_MODEL
SSD_HEAD_DIM = 64
SSD_HEADS = SSD_INNER // SSD_HEAD_DIM
SSD_GROUPS = 4
SSD_STATE = 128
SSD_CONV = 4
SSD_CONV_DIM = SSD_INNER + 2 * SSD_GROUPS * SSD_STATE

FFN_HIDDEN = 2816
FFN_CONV = 3

N_BRANCH = 2

Q_COLS = ATT_HEADS * 2 * ATT_HEAD_DIM
K_COLS = ATT_HEADS * 2 * ATT_HEAD_DIM
V_COLS = ATT_WIDTH
Z_COLS = SSD_INNER
XBC_COLS = SSD_CONV_DIM
DT_COLS = SSD_HEADS
GATE_COLS = N_BRANCH * D_MODEL
IN_COLS = Q_COLS + K_COLS + V_COLS + Z_COLS + XBC_COLS + DT_COLS + GATE_COLS

kernel_name = "hybrid_diffattn_ssd_convffn_block"


def rmsnorm(x, w):
    xf = x.astype(jnp.float32)
    y = xf * lax.rsqrt(jnp.mean(xf * xf, axis=-1, keepdims=True) + EPS)
    return (y * w.astype(jnp.float32)).astype(x.dtype)


def causal_dwconv(u, w, b):
    ksz = w.shape[0]
    y = lax.conv_general_dilated(u, w[:, None, :].astype(u.dtype), window_strides=(1,),
                                 padding=[(ksz - 1, 0)],
                                 dimension_numbers=('NWC', 'WIO', 'NWC'),
                                 feature_group_count=u.shape[-1])
    return y + b.astype(u.dtype)


def alibi_slopes(n_heads):
    idx = jnp.arange(1, n_heads + 1, dtype=jnp.float32)
    return jnp.exp2(-8.0 * idx / n_heads)


def diff_attention(q, k, v, lam, lam_init, subln_w):
    bsz, seq = q.shape[:2]
    nq = seq // Q_BLOCK
    scale = ATT_HEAD_DIM ** -0.5
    qb = q.reshape(bsz, nq, Q_BLOCK, ATT_HEADS, 2, ATT_HEAD_DIM).transpose(1, 0, 3, 4, 2, 5)
    kt = k.transpose(0, 2, 3, 1, 4)
    vt = v.transpose(0, 2, 1, 3).astype(jnp.float32)
    slopes = alibi_slopes(ATT_HEADS)
    kpos = jnp.arange(seq)
    sub_w = subln_w.astype(jnp.float32)

    def block(args):
        i, qi = args
        qpos = i * Q_BLOCK + jnp.arange(Q_BLOCK)
        s = jnp.einsum('bhmqd,bhmkd->bhmqk', qi, kt).astype(jnp.float32) * scale
        dist = jnp.abs(qpos[:, None] - kpos[None, :]).astype(jnp.float32)
        bias = -slopes[:, None, None] * dist
        allowed = (kpos[None, :] // CHUNK) <= (qpos[:, None] // CHUNK)
        s = jnp.where(allowed, s + bias[None, :, None], -jnp.inf)
        p = jax.nn.softmax(s, axis=-1)
        a = p[:, :, 0] - lam * p[:, :, 1]
        o = jnp.einsum('bhqk,bhkd->bhqd', a, vt)
        o = o * lax.rsqrt(jnp.mean(o * o, axis=-1, keepdims=True) + EPS) * sub_w * (1.0 - lam_init)
        return o.astype(v.dtype)

    out = lax.map(block, (jnp.arange(nq), qb))
    return out.transpose(1, 0, 3, 2, 4).reshape(bsz, seq, ATT_HEADS * ATT_V_DIM)


def segsum(a):
    L = a.shape[-1]
    ar = jnp.broadcast_to(a[..., :, None], a.shape + (L,))
    strict = jnp.tril(jnp.ones((L, L), dtype=bool), -1)
    cs = jnp.cumsum(jnp.where(strict, ar, 0.0), axis=-2)
    incl = jnp.tril(jnp.ones((L, L), dtype=bool), 0)
    return jnp.where(incl, cs, -jnp.inf)


def ssd_mixer(z, xbc, dt_raw, conv_w, conv_b, dt_bias, a_log, d_skip, norm_w):
    bsz, seq, _ = xbc.shape
    nc = seq // CHUNK
    G, R, P, N = SSD_GROUPS, SSD_HEADS // SSD_GROUPS, SSD_HEAD_DIM, SSD_STATE
    xbc = jax.nn.silu(causal_dwconv(xbc, conv_w, conv_b))
    xs, b_in, c_in = jnp.split(xbc, [SSD_INNER, SSD_INNER + G * N], axis=-1)
    dt = jax.nn.softplus(dt_raw.astype(jnp.float32) + dt_bias.astype(jnp.float32))
    A = -jnp.exp(a_log.astype(jnp.float32))
    xh = xs.reshape(bsz, nc, CHUNK, G, R, P).astype(jnp.float32)
    X = xh * dt.reshape(bsz, nc, CHUNK, G, R)[..., None]
    Bc = b_in.reshape(bsz, nc, CHUNK, G, N).astype(jnp.float32)
    Cc = c_in.reshape(bsz, nc, CHUNK, G, N).astype(jnp.float32)
    a = (dt * A).reshape(bsz, nc, CHUNK, G, R).transpose(0, 3, 4, 1, 2)
    a_cum = jnp.cumsum(a, axis=-1)
    Lmat = jnp.exp(segsum(a))
    cb = jnp.einsum('bclgn,bcsgn->bgcls', Cc, Bc)
    y_diag = jnp.einsum('bgrcls,bcsgrp->bclgrp', cb[:, :, None] * Lmat, X)
    decay_states = jnp.exp(a_cum[..., -1:] - a_cum).transpose(0, 3, 4, 1, 2)
    states = jnp.einsum('bclgn,bclgrp->bcgrpn', Bc, X * decay_states[..., None])
    chunk_decay = jnp.exp(a_cum[..., -1]).transpose(3, 0, 1, 2)

    def step(h, inp):
        st, dec = inp
        return h * dec[..., None, None] + st, h

    h0 = jnp.zeros((bsz, G, R, P, N), jnp.float32)
    _, prev = lax.scan(step, h0, (states.transpose(1, 0, 2, 3, 4, 5), chunk_decay))
    out_decay = jnp.exp(a_cum).transpose(0, 3, 4, 1, 2)
    y_off = jnp.einsum('bclgn,cbgrpn->bclgrp', Cc, prev) * out_decay[..., None]
    y = y_diag + y_off + xh * d_skip.astype(jnp.float32).reshape(G, R, 1)
    y = y.reshape(bsz, seq, SSD_INNER)
    gated = (y * jax.nn.silu(z.astype(jnp.float32))).reshape(bsz, seq, G, SSD_INNER // G)
    gated = gated * lax.rsqrt(jnp.mean(gated * gated, axis=-1, keepdims=True) + EPS)
    return (gated.reshape(bsz, seq, SSD_INNER) * norm_w.astype(jnp.float32)).astype(z.dtype)


def setup_inputs(seed: int = 0) -> dict:
    key = jax.random.key(seed)
    ks = jax.random.split(key, 26)

    def nrm(k, shape, scale):
        return jax.random.normal(k, shape, jnp.float32) * scale

    dt0 = jnp.exp(jax.random.uniform(ks[11], (DEPTH, SSD_HEADS), jnp.float32,
                                     minval=math.log(1e-3), maxval=math.log(1e-1)))
    return {
        'x': nrm(ks[0], (BATCH, SEQ, D_MODEL), 1.0),
        'mix_norm_w': 1.0 + nrm(ks[1], (DEPTH, D_MODEL), 0.02),
        'w_in': nrm(ks[2], (DEPTH, D_MODEL, IN_COLS), D_MODEL ** -0.5),
        'gate_b': nrm(ks[3], (DEPTH, GATE_COLS), 0.02),
        'lambda_q1': nrm(ks[4], (DEPTH, ATT_HEAD_DIM), 0.1),
        'lambda_k1': nrm(ks[5], (DEPTH, ATT_HEAD_DIM), 0.1),
        'lambda_q2': nrm(ks[6], (DEPTH, ATT_HEAD_DIM), 0.1),
        'lambda_k2': nrm(ks[7], (DEPTH, ATT_HEAD_DIM), 0.1),
        'attn_subln_w': 1.0 + nrm(ks[8], (DEPTH, ATT_V_DIM), 0.02),
        'ssd_conv_w': nrm(ks[9], (DEPTH, SSD_CONV, SSD_CONV_DIM), SSD_CONV ** -0.5),
        'ssd_conv_b': nrm(ks[10], (DEPTH, SSD_CONV_DIM), 0.02),
        'ssd_dt_bias': dt0 + jnp.log(-jnp.expm1(-dt0)),
        'ssd_a_log': jnp.log(jax.random.uniform(ks[12], (DEPTH, SSD_HEADS), jnp.float32, minval=1.0, maxval=16.0)),
        'ssd_d': 1.0 + nrm(ks[13], (DEPTH, SSD_HEADS), 0.02),
        'ssd_norm_w': 1.0 + nrm(ks[14], (DEPTH, SSD_INNER), 0.02),
        'w_attn_branch': nrm(ks[15], (DEPTH, ATT_WIDTH, D_MODEL), ATT_WIDTH ** -0.5),
        'w_ssd_branch': nrm(ks[16], (DEPTH, SSD_INNER, D_MODEL), SSD_INNER ** -0.5),
        'w_out': nrm(ks[17], (DEPTH, D_MODEL, D_MODEL), D_MODEL ** -0.5),
        'ffn_norm_w': 1.0 + nrm(ks[18], (DEPTH, D_MODEL), 0.02),
        'w_up': nrm(ks[19], (DEPTH, D_MODEL, 2 * FFN_HIDDEN), D_MODEL ** -0.5),
        'ffn_conv_w': nrm(ks[20], (DEPTH, FFN_CONV, 2 * FFN_HIDDEN), FFN_CONV ** -0.5),
        'ffn_conv_b': nrm(ks[21], (DEPTH, 2 * FFN_HIDDEN), 0.02),
        'w_down': nrm(ks[22], (DEPTH, FFN_HIDDEN, D_MODEL), FFN_HIDDEN ** -0.5),
        'final_norm_w': 1.0 + nrm(ks[23], (D_MODEL,), 0.02),
    }


def reference(x, mix_norm_w, w_in, gate_b, lambda_q1, lambda_k1, lambda_q2, lambda_k2,
              attn_subln_w, ssd_conv_w, ssd_conv_b, ssd_dt_bias, ssd_a_log, ssd_d, ssd_norm_w,
              w_attn_branch, w_ssd_branch, w_out, ffn_norm_w, w_up, ffn_conv_w, ffn_conv_b,
              w_down, final_norm_w):
    bsz, seq, _ = x.shape
    cuts = [int(c) for c in np.cumsum([Q_COLS, K_COLS, V_COLS, Z_COLS, XBC_COLS, DT_COLS])]
    res = x
    for l in range(DEPTH):
        h = rmsnorm(res, mix_norm_w[l])
        proj = h @ w_in[l]
        q, k, v, z, xbc, dt_raw, gate_pre = jnp.split(proj, cuts, axis=-1)
        lam_init = 0.8 - 0.6 * math.exp(-0.3 * l)
        lam = (jnp.exp(jnp.sum(lambda_q1[l].astype(jnp.float32) * lambda_k1[l].astype(jnp.float32)))
               - jnp.exp(jnp.sum(lambda_q2[l].astype(jnp.float32) * lambda_k2[l].astype(jnp.float32)))
               + lam_init)
        y_att = diff_attention(q.reshape(bsz, seq, ATT_HEADS, 2, ATT_HEAD_DIM),
                               k.reshape(bsz, seq, ATT_HEADS, 2, ATT_HEAD_DIM),
                               v.reshape(bsz, seq, ATT_HEADS, ATT_V_DIM),
                               lam, lam_init, attn_subln_w[l])
        y_ssd = ssd_mixer(z, xbc, dt_raw, ssd_conv_w[l], ssd_conv_b[l], ssd_dt_bias[l],
                          ssd_a_log[l], ssd_d[l], ssd_norm_w[l])
        gates = jax.nn.sigmoid(gate_pre + gate_b[l])
        g_att, g_ssd = jnp.split(gates, 2, axis=-1)
        merged = g_att * (y_att @ w_attn_branch[l]) + g_ssd * (y_ssd @ w_ssd_branch[l])
        res = res + merged @ w_out[l]
        h = rmsnorm(res, ffn_norm_w[l])
        u = causal_dwconv(h @ w_up[l], ffn_conv_w[l], ffn_conv_b[l])
        u_val, u_gate = jnp.split(u, 2, axis=-1)
        res = res + (jax.nn.silu(u_gate) * u_val) @ w_down[l]
    return rmsnorm(res, final_norm_w)
```

```python
import functools
import math

import jax
import jax.numpy as jnp
from jax import lax
from jax.experimental import pallas as pl
from jax.experimental.pallas import tpu as pltpu

F32 = jnp.float32
BF16 = jnp.bfloat16

EPS = 1e-6
LOG2E = math.log2(math.e)
NEG_BIG = -1e30

ATT_HEADS = 8
ATT_HEAD_DIM = 64
ATT_V_DIM = 2 * ATT_HEAD_DIM
ATT_CHUNK = 64
SSD_HEADS = 32
SSD_HEAD_DIM = 64
SSD_GROUPS = 4
SSD_STATE = 128
SSD_CONV = 4
FFN_CONV = 3

LANES = 128
VMEM_LIMIT_BYTES = 56 * 1024 * 1024

ATT_TILE = 256
SSD_TILE = 128
ROW_TILE = 512
MM_ROWS = 1024
MM_COLS = 1024
FFN_SPLIT = 2
HALO = 16


def _cparams(*sem):
    return pltpu.CompilerParams(dimension_semantics=sem, vmem_limit_bytes=VMEM_LIMIT_BYTES)


def _sigmoid(x):
    return 1.0 / (1.0 + jnp.exp(-x))


def _silu(x):
    return x * _sigmoid(x)


def _rmsnorm(x, w):
    ms = jnp.mean(x * x, axis=-1, keepdims=True)
    return x * lax.rsqrt(ms + EPS) * w


def _split3(x):
    hi = x.astype(BF16)
    r1 = x - hi.astype(F32)
    mid = r1.astype(BF16)
    lo = (r1 - mid.astype(F32)).astype(BF16)
    return hi, mid, lo


def _norm_dt_kernel(x_ref, w_ref, wdt_ref, h_ref, dt_ref):
    h = _rmsnorm(x_ref[...], w_ref[...]).astype(BF16)
    h_ref[...] = h
    dt_ref[...] = jnp.dot(h, wdt_ref[...], preferred_element_type=F32)


def _norm_dt(x, w, wdt, tm):
    t, d = x.shape
    return pl.pallas_call(
        _norm_dt_kernel,
        grid=(t // tm,),
        in_specs=[pl.BlockSpec((tm, d), lambda i: (i, 0)),
                  pl.BlockSpec((1, d), lambda i: (0, 0)),
                  pl.BlockSpec((d, LANES), lambda i: (0, 0))],
        out_specs=[pl.BlockSpec((tm, d), lambda i: (i, 0)),
                   pl.BlockSpec((tm, LANES), lambda i: (i, 0))],
        out_shape=[jax.ShapeDtypeStruct((t, d), BF16),
                   jax.ShapeDtypeStruct((t, LANES), F32)],
        compiler_params=_cparams("parallel"),
        name="norm_dt",
    )(x, w, wdt)


def _mm_kernel(a_ref, b_ref, o_ref):
    o_ref[...] = jnp.dot(a_ref[...], b_ref[...], preferred_element_type=F32).astype(o_ref.dtype)


def _matmul(a, b, tm, tn, name):
    m, k = a.shape
    n = b.shape[1]
    return pl.pallas_call(
        _mm_kernel,
        grid=(m // tm, n // tn),
        in_specs=[pl.BlockSpec((tm, k), lambda i, j: (i, 0)),
                  pl.BlockSpec((k, tn), lambda i, j: (0, j))],
        out_specs=pl.BlockSpec((tm, tn), lambda i, j: (i, j)),
        out_shape=jax.ShapeDtypeStruct((m, n), BF16),
        compiler_params=_cparams("parallel", "parallel"),
        name=name,
    )(a, b)


def _proj_t_kernel(wt_ref, h_ref, o_ref):
    o_ref[0, 0] = lax.dot_general(wt_ref[...], h_ref[...], (((1,), (1,)), ((), ())),
                                  preferred_element_type=F32).astype(BF16)


def _proj_t(wt, h, bsz, seq, tq):
    n, d = wt.shape
    nq = seq // tq
    return pl.pallas_call(
        _proj_t_kernel,
        grid=(bsz, nq),
        in_specs=[pl.BlockSpec((n, d), lambda b, i: (0, 0)),
                  pl.BlockSpec((tq, d), lambda b, i: (b * nq + i, 0))],
        out_specs=pl.BlockSpec((1, 1, n, tq), lambda b, i: (b, i, 0, 0)),
        out_shape=jax.ShapeDtypeStruct((bsz, nq, n, tq), BF16),
        compiler_params=_cparams("parallel", "parallel"),
        name="proj_t",
    )(wt, h)


def _attn_kernel(slopes_ref, lamv_ref, subw_ref, qt_ref, k_ref, vt_ref, o_ref, acc_ref, ml_ref,
                 *, tq, lam_init):
    hd = pl.program_id(1)
    qi = pl.program_id(2)
    a = slopes_ref[hd]

    qt = qt_ref[0, 0]
    row = lax.broadcasted_iota(jnp.int32, (ATT_V_DIM, tq), 0)
    zero = jnp.zeros_like(qt)
    ridx = lax.broadcasted_iota(jnp.int32, (8, tq), 0)
    rpos = lax.broadcasted_iota(jnp.int32, (8, tq), 1).astype(F32)
    val = jnp.where(ridx < 3, a, -a * rpos)
    hi = val.astype(BF16).astype(F32)
    r1 = val - hi
    mid = r1.astype(BF16).astype(F32)
    lo = r1 - mid
    sel = ridx % 3
    piece = jnp.where(sel == 0, hi, jnp.where(sel == 1, mid, lo))
    piece = jnp.where(ridx < 6, piece, 0.0)
    aug = jnp.concatenate([piece, jnp.zeros((LANES - 8, tq), F32)], axis=0).astype(BF16)
    rhs = (jnp.concatenate([jnp.where(row < ATT_HEAD_DIM, qt, zero), aug], axis=0),
           jnp.concatenate([jnp.where(row >= ATT_HEAD_DIM, qt, zero), aug], axis=0))

    cpos = lax.broadcasted_iota(jnp.int32, (tq, LANES), 0).astype(F32)
    lane = lax.broadcasted_iota(jnp.int32, (tq, LANES), 1)
    kaug = jnp.where(lane < 3, cpos, jnp.where(lane < 6, 1.0, 0.0)).astype(BF16)

    acc_ref[...] = jnp.zeros_like(acc_ref)
    ml_ref[...] = jnp.where(lax.broadcasted_iota(jnp.int32, ml_ref.shape, 0) % 2 == 0, NEG_BIG, 0.0)

    def tile_step(j, cj, extra):
        start = pl.multiple_of(j * tq, tq)
        kf = jnp.concatenate([k_ref[0, pl.ds(start, tq), :], kaug], axis=1)
        vt = vt_ref[0, j]
        for mi in range(2):
            s = jnp.dot(kf, rhs[mi], preferred_element_type=F32)
            if extra is not None:
                s = s + extra
            m_old = ml_ref[2 * mi:2 * mi + 1, :]
            l_old = ml_ref[2 * mi + 1:2 * mi + 2, :]
            m_new = jnp.maximum(m_old, jnp.max(s, axis=0, keepdims=True) + cj)
            alpha = jnp.exp2(m_old - m_new)
            p = jnp.exp2(s - (m_new - cj))
            ml_ref[2 * mi + 1:2 * mi + 2, :] = alpha * l_old + jnp.sum(p, axis=0, keepdims=True)
            ml_ref[2 * mi:2 * mi + 1, :] = m_new
            acc_ref[mi] = alpha * acc_ref[mi] + jnp.dot(vt, p.astype(BF16),
                                                        preferred_element_type=F32)

    def body(j, carry):
        tile_step(j, -a * ((qi - j) * tq).astype(F32), None)
        return carry

    lax.fori_loop(0, qi, body, 0)

    ci = lax.broadcasted_iota(jnp.int32, (tq, tq), 0)
    ri = lax.broadcasted_iota(jnp.int32, (tq, tq), 1)
    allowed = (ci // ATT_CHUNK) <= (ri // ATT_CHUNK)
    fix = jnp.where(ci > ri, (2.0 * a) * (ri - ci).astype(F32), 0.0)
    tile_step(qi, 0.0, jnp.where(allowed, fix, NEG_BIG))

    lv = lamv_ref[...]
    t1 = jnp.sum(lv[0:1] * lv[1:2], axis=1, keepdims=True)
    t2 = jnp.sum(lv[2:3] * lv[3:4], axis=1, keepdims=True)
    lam = jnp.exp(t1) - jnp.exp(t2) + lam_init
    o = acc_ref[0] * (1.0 / ml_ref[1:2, :]) - lam * (acc_ref[1] * (1.0 / ml_ref[3:4, :]))
    o = o * lax.rsqrt(jnp.mean(o * o, axis=0, keepdims=True) + EPS)
    o_ref[0] = (o.T * (subw_ref[...] * (1.0 - lam_init))).astype(BF16)


def _attention(slopes, lamv, subw, qvt, k, bsz, seq, tq, lam_init):
    nq = seq // tq
    d = ATT_HEADS * ATT_V_DIM
    return pl.pallas_call(
        functools.partial(_attn_kernel, tq=tq, lam_init=lam_init),
        grid=(bsz, ATT_HEADS, nq),
        in_specs=[pl.BlockSpec(memory_space=pltpu.SMEM),
                  pl.BlockSpec((4, ATT_HEAD_DIM), lambda b, h, i: (0, 0)),
                  pl.BlockSpec((1, ATT_V_DIM), lambda b, h, i: (0, 0)),
                  pl.BlockSpec((1, 1, ATT_V_DIM, tq), lambda b, h, i: (b, i, h, 0)),
                  pl.BlockSpec((1, seq, ATT_V_DIM), lambda b, h, i: (b, 0, h)),
                  pl.BlockSpec((1, nq, ATT_V_DIM, tq), lambda b, h, i: (b, 0, ATT_HEADS + h, 0))],
        out_specs=pl.BlockSpec((1, tq, ATT_V_DIM), lambda b, h, i: (b, i, h)),
        out_shape=jax.ShapeDtypeStruct((bsz, seq, d), BF16),
        scratch_shapes=[pltpu.VMEM((2, ATT_V_DIM, tq), F32), pltpu.VMEM((4, tq), F32)],
        compiler_params=_cparams("parallel", "parallel", "arbitrary"),
        name="attn",
    )(slopes, lamv, subw, qvt, k, qvt)


def _ssd_kernel(z_ref, xbc_ref, dt_ref, cw_ref, cb_ref, dtb_ref, alog_ref, dexp_ref, nw_ref,
                y_ref, h_scr, halo_scr, *, lc):
    inner = SSD_HEADS * SSD_HEAD_DIM
    gw = inner // SSD_GROUPS
    bc0 = inner
    cc0 = inner + SSD_GROUPS * SSD_STATE

    @pl.when(pl.program_id(1) == 0)
    def _():
        h_scr[...] = jnp.zeros_like(h_scr)
        halo_scr[...] = jnp.zeros_like(halo_scr)

    xraw = xbc_ref[0].astype(F32)
    xext = jnp.concatenate([halo_scr[...], xraw], axis=0)
    halo_scr[...] = xraw[lc - 8:, :]
    cw = cw_ref[...]
    conv = cb_ref[...]
    for kk in range(SSD_CONV):
        off = 8 - (SSD_CONV - 1) + kk
        conv = conv + cw[kk:kk + 1] * xext[off:off + lc]
    xc = _silu(conv)
    xs = xc[:, :inner]

    x = dt_ref[0] + dtb_ref[...]
    dt = jnp.maximum(x, 0.0) + jnp.log(1.0 + jnp.exp(-jnp.abs(x)))
    a = dt * (-jnp.exp(alog_ref[...]))

    ii = lax.broadcasted_iota(jnp.int32, (lc, lc), 0)
    jj = lax.broadcasted_iota(jnp.int32, (lc, lc), 1)
    tril = jj <= ii
    tri = jnp.where(tril, 1.0, 0.0).astype(BF16)
    a3 = jnp.concatenate(_split3(a), axis=1)
    c3 = jnp.dot(tri, a3, preferred_element_type=F32)
    acum = c3[:, :LANES] + c3[:, LANES:2 * LANES] + c3[:, 2 * LANES:]

    er = lax.broadcasted_iota(jnp.int32, (LANES, inner), 0)
    ec = lax.broadcasted_iota(jnp.int32, (LANES, inner), 1)
    expand = jnp.where(ec // SSD_HEAD_DIM == er, 1.0, 0.0).astype(BF16)
    dt_hi = dt.astype(BF16)
    dt_lo = (dt - dt_hi.astype(F32)).astype(BF16)
    stack = jnp.concatenate(list(_split3(acum)) + [dt_hi, dt_lo], axis=0)
    ex = jnp.dot(stack, expand, preferred_element_type=F32)
    acum_e = ex[:lc] + ex[lc:2 * lc] + ex[2 * lc:3 * lc]
    dt_e = ex[3 * lc:4 * lc] + ex[4 * lc:]

    acum_last = acum_e[lc - 1:lc, :]
    xd = xs * dt_e
    xd_b = xd.astype(BF16)
    xd_end = (xd * jnp.exp(acum_last - acum_e)).astype(BF16)
    dec_in = jnp.exp(acum_e)
    dec_all = jnp.exp(acum_last)
    acum_t = acum.T

    lane = lax.broadcasted_iota(jnp.int32, (lc, LANES), 1)
    y_parts = []
    for g in range(SSD_GROUPS):
        bg = xc[:, bc0 + g * SSD_STATE:bc0 + (g + 1) * SSD_STATE]
        cg = xc[:, cc0 + g * SSD_STATE:cc0 + (g + 1) * SSD_STATE].astype(BF16)
        gsl = slice(g * gw, (g + 1) * gw)
        cbt = lax.dot_general(cg, bg.astype(BF16), (((1,), (1,)), ((), ())),
                              preferred_element_type=F32)
        h_prev = h_scr[:, gsl]
        y_off = jnp.dot(cg, h_prev.astype(BF16), preferred_element_type=F32) * dec_in[:, gsl]
        st = jnp.dot(bg.T.astype(BF16), xd_end[:, gsl], preferred_element_type=F32)
        h_scr[:, gsl] = h_prev * dec_all[:, gsl] + st
        heads_per_group = SSD_HEADS // SSD_GROUPS
        for pr in range(heads_per_group // 2):
            r0 = g * heads_per_group + 2 * pr
            ms = []
            for r in (r0, r0 + 1):
                seg = acum[:, r:r + 1] - acum_t[r:r + 1, :]
                ms.append((cbt * jnp.exp(jnp.where(tril, seg, NEG_BIG))).astype(BF16))
            xp = xd_b[:, r0 * SSD_HEAD_DIM:r0 * SSD_HEAD_DIM + LANES]
            zero = jnp.zeros_like(xp)
            xblk = jnp.concatenate([jnp.where(lane < SSD_HEAD_DIM, xp, zero),
                                    jnp.where(lane >= SSD_HEAD_DIM, xp, zero)], axis=0)
            y_diag = jnp.dot(jnp.concatenate(ms, axis=1), xblk, preferred_element_type=F32)
            lsl = slice(pr * LANES, (pr + 1) * LANES)
            y_parts.append(y_diag + y_off[:, lsl])
    y = jnp.concatenate(y_parts, axis=1) + xs * dexp_ref[...]

    gated = y * _silu(z_ref[0].astype(F32))
    outs = []
    for g in range(SSD_GROUPS):
        gg = gated[:, g * gw:(g + 1) * gw]
        outs.append(gg * lax.rsqrt(jnp.mean(gg * gg, axis=-1, keepdims=True) + EPS))
    y_ref[0] = (jnp.concatenate(outs, axis=1) * nw_ref[...]).astype(BF16)


def _ssd(z, xbc, dt, cw, cb, dtb, alog, dexp, nw, lc):
    bsz, seq, inner = z.shape
    cdim = xbc.shape[-1]
    full = lambda b, c: (0, 0)
    return pl.pallas_call(
        functools.partial(_ssd_kernel, lc=lc),
        grid=(bsz, seq // lc),
        in_specs=[pl.BlockSpec((1, lc, inner), lambda b, c: (b, c, 0)),
                  pl.BlockSpec((1, lc, cdim), lambda b, c: (b, c, 0)),
                  pl.BlockSpec((1, lc, LANES), lambda b, c: (b, c, 0)),
                  pl.BlockSpec((SSD_CONV, cdim), full),
                  pl.BlockSpec((1, cdim), full),
                  pl.BlockSpec((1, LANES), full),
                  pl.BlockSpec((1, LANES), full),
                  pl.BlockSpec((1, inner), full),
                  pl.BlockSpec((1, inner), full)],
        out_specs=pl.BlockSpec((1, lc, inner), lambda b, c: (b, c, 0)),
        out_shape=jax.ShapeDtypeStruct((bsz, seq, inner), BF16),
        scratch_shapes=[pltpu.VMEM((SSD_STATE, inner), F32), pltpu.VMEM((8, cdim), F32)],
        compiler_params=_cparams("parallel", "arbitrary"),
        name="ssd",
    )(z, xbc, dt, cw, cb, dtb, alog, dexp, nw)


def _merge_kernel(x_ref, ya_ref, ys_ref, gp_ref, gb_ref, wa_ref, ws_ref, wo_ref, o_ref):
    d = x_ref.shape[-1]
    gates = _sigmoid(gp_ref[...].astype(F32) + gb_ref[...])
    pa = jnp.dot(ya_ref[...], wa_ref[...], preferred_element_type=F32)
    ps = jnp.dot(ys_ref[...], ws_ref[...], preferred_element_type=F32)
    merged = gates[:, :d] * pa + gates[:, d:] * ps
    o_ref[...] = x_ref[...] + jnp.dot(merged.astype(BF16), wo_ref[...], preferred_element_type=F32)


def _merge(x, ya, ys, gp, gb, wa, ws, wo, tm):
    t, d = x.shape
    row = lambda i: (i, 0)
    full = lambda i: (0, 0)
    return pl.pallas_call(
        _merge_kernel,
        grid=(t // tm,),
        in_specs=[pl.BlockSpec((tm, d), row),
                  pl.BlockSpec((tm, ya.shape[1]), row),
                  pl.BlockSpec((tm, ys.shape[1]), row),
                  pl.BlockSpec((tm, gp.shape[1]), row),
                  pl.BlockSpec(gb.shape, full),
                  pl.BlockSpec(wa.shape, full),
                  pl.BlockSpec(ws.shape, full),
                  pl.BlockSpec(wo.shape, full)],
        out_specs=pl.BlockSpec((tm, d), row),
        out_shape=jax.ShapeDtypeStruct((t, d), F32),
        compiler_params=_cparams("parallel"),
        name="merge",
    )(x, ya, ys, gp, gb, wa, ws, wo)


def _ffn_kernel(res_ref, halo_ref, nw_ref, wv_ref, wg_ref, cwv_ref, cwg_ref, cbv_ref, cbg_ref,
                wd_ref, fnw_ref, o_ref, h_scr, acc_scr, *, tm, tiles_per_seq, final_norm):
    i = pl.program_id(0)
    j = pl.program_id(1)

    @pl.when(j == 0)
    def _():
        nw = nw_ref[...]
        hh = _rmsnorm(halo_ref[...], nw)
        hh = jnp.where(i % tiles_per_seq == 0, 0.0, hh)
        h_scr[0:HALO, :] = hh.astype(BF16)
        h_scr[HALO:, :] = _rmsnorm(res_ref[...], nw).astype(BF16)

    hx = h_scr[...]

    def branch(w_ref, cw_ref, cb_ref):
        u = jnp.dot(hx, w_ref[...], preferred_element_type=F32)
        cw = cw_ref[...]
        out = cb_ref[...]
        for kk in range(FFN_CONV):
            off = HALO - (FFN_CONV - 1) + kk
            out = out + cw[kk:kk + 1] * u[off:off + tm]
        return out

    act = (_silu(branch(wg_ref, cwg_ref, cbg_ref)) * branch(wv_ref, cwv_ref, cbv_ref)).astype(BF16)
    part = jnp.dot(act, wd_ref[...], preferred_element_type=F32)

    @pl.when(j == 0)
    def _():
        acc_scr[...] = part

    @pl.when(j > 0)
    def _():
        acc_scr[...] += part

    @pl.when(j == pl.num_programs(1) - 1)
    def _():
        r = res_ref[...] + acc_scr[...]
        o_ref[...] = _rmsnorm(r, fnw_ref[...]) if final_norm else r


def _ffn(res, nw, w_up, cw, cb, w_down, fnw, seq, tm, final_norm):
    t, d = res.shape
    hid = w_down.shape[0]
    th = hid // FFN_SPLIT
    nsplit = FFN_SPLIT
    row = lambda i, j: (i, 0)
    full = lambda i, j: (0, 0)
    halo_blocks = tm // HALO
    return pl.pallas_call(
        functools.partial(_ffn_kernel, tm=tm, tiles_per_seq=seq // tm, final_norm=final_norm),
        grid=(t // tm, nsplit),
        in_specs=[pl.BlockSpec((tm, d), row),
                  pl.BlockSpec((HALO, d), lambda i, j: (jnp.maximum(i * halo_blocks - 1, 0), 0)),
                  pl.BlockSpec((1, d), full),
                  pl.BlockSpec((d, th), lambda i, j: (0, j)),
                  pl.BlockSpec((d, th), lambda i, j: (0, nsplit + j)),
                  pl.BlockSpec((FFN_CONV, th), lambda i, j: (0, j)),
                  pl.BlockSpec((FFN_CONV, th), lambda i, j: (0, nsplit + j)),
                  pl.BlockSpec((1, th), lambda i, j: (0, j)),
                  pl.BlockSpec((1, th), lambda i, j: (0, nsplit + j)),
                  pl.BlockSpec((th, d), lambda i, j: (j, 0)),
                  pl.BlockSpec((1, d), full)],
        out_specs=pl.BlockSpec((tm, d), row),
        out_shape=jax.ShapeDtypeStruct((t, d), F32),
        scratch_shapes=[pltpu.VMEM((HALO + tm, d), BF16), pltpu.VMEM((tm, d), F32)],
        compiler_params=_cparams("parallel", "arbitrary"),
        name="ffn",
    )(res, res, nw, w_up, w_up, cw, cw, cb, cb, w_down, fnw)


def kernel(x, mix_norm_w, w_in, gate_b, lambda_q1, lambda_k1, lambda_q2, lambda_k2, attn_subln_w,
           ssd_conv_w, ssd_conv_b, ssd_dt_bias, ssd_a_log, ssd_d, ssd_norm_w, w_attn_branch,
           w_ssd_branch, w_out, ffn_norm_w, w_up, ffn_conv_w, ffn_conv_b, w_down, final_norm_w):
    bsz, seq, d = x.shape
    t = bsz * seq
    depth = w_in.shape[0]
    att_w = ATT_HEADS * ATT_V_DIM
    inner = SSD_HEADS * SSD_HEAD_DIM
    cdim = inner + 2 * SSD_GROUPS * SSD_STATE
    c_q, c_k, c_v, c_z, c_x, c_dt = (att_w, 2 * att_w, 3 * att_w, 3 * att_w + inner,
                                     3 * att_w + inner + cdim, 3 * att_w + inner + cdim + SSD_HEADS)
    tq = min(ATT_TILE, seq)
    lc = min(SSD_TILE, seq)
    tm = min(ROW_TILE, seq)
    mm_rows = min(MM_ROWS, t)
    slopes = (jnp.exp2(-8.0 * jnp.arange(1, ATT_HEADS + 1, dtype=F32) / ATT_HEADS) * LOG2E).astype(F32)
    pad_heads = lambda v: jnp.pad(v.astype(F32), (0, LANES - SSD_HEADS)).reshape(1, LANES)

    res = x.reshape(t, d)
    for l in range(depth):
        w = w_in[l]
        lam_init = 0.8 - 0.6 * math.exp(-0.3 * l)
        wq = w[:, :c_q] * (ATT_HEAD_DIM ** -0.5 * LOG2E)
        wqv_t = jnp.concatenate([wq, w[:, c_k:c_v]], axis=1).T.astype(BF16)
        wdt = jnp.pad(w[:, c_x:c_dt], ((0, 0), (0, LANES - SSD_HEADS))).astype(BF16)

        h, dt_raw = _norm_dt(res, mix_norm_w[l].reshape(1, d), wdt, tm)
        k = _matmul(h, w[:, c_q:c_k].astype(BF16), mm_rows, MM_COLS, "proj_k")
        z = _matmul(h, w[:, c_v:c_z].astype(BF16), mm_rows, MM_COLS, "proj_z")
        xbc = _matmul(h, w[:, c_z:c_x].astype(BF16), mm_rows, MM_COLS, "proj_xbc")
        gate_pre = _matmul(h, w[:, c_dt:].astype(BF16), mm_rows, MM_COLS, "proj_gate")
        qvt = _proj_t(wqv_t, h, bsz, seq, tq)

        lamv = jnp.stack([lambda_q1[l], lambda_k1[l], lambda_q2[l], lambda_k2[l]]).astype(F32)
        y_att = _attention(slopes, lamv, attn_subln_w[l].reshape(1, ATT_V_DIM).astype(F32), qvt,
                           k.reshape(bsz, seq, att_w), bsz, seq, tq, lam_init)
        y_ssd = _ssd(z.reshape(bsz, seq, inner), xbc.reshape(bsz, seq, cdim),
                     dt_raw.reshape(bsz, seq, LANES), ssd_conv_w[l].astype(F32),
                     ssd_conv_b[l].reshape(1, cdim).astype(F32), pad_heads(ssd_dt_bias[l]),
                     pad_heads(ssd_a_log[l]),
                     jnp.repeat(ssd_d[l].astype(F32), SSD_HEAD_DIM).reshape(1, inner),
                     ssd_norm_w[l].reshape(1, inner).astype(F32), lc)
        res = _merge(res, y_att.reshape(t, att_w), y_ssd.reshape(t, inner), gate_pre,
                     gate_b[l].reshape(1, -1).astype(F32), w_attn_branch[l].astype(BF16),
                     w_ssd_branch[l].astype(BF16), w_out[l].astype(BF16), tm)
        res = _ffn(res, ffn_norm_w[l].reshape(1, d).astype(F32), w_up[l].astype(BF16),
                   ffn_conv_w[l].astype(F32), ffn_conv_b[l].reshape(1, -1).astype(F32),
                   w_down[l].astype(BF16), final_norm_w.reshape(1, d).astype(F32), seq, tm,
                   l == depth - 1)
    return res.reshape(bsz, seq, d)
```

```python
import functools
import math

import jax
import jax.numpy as jnp
from jax import lax
from jax.experimental import pallas as pl
from jax.experimental.pallas import tpu as pltpu

F32 = jnp.float32
BF16 = jnp.bfloat16

EPS = 1e-6
LOG2E = math.log2(math.e)
NEG_BIG = -1e30

ATT_HEADS = 8
ATT_HEAD_DIM = 64
ATT_V_DIM = 2 * ATT_HEAD_DIM
ATT_CHUNK = 64
SSD_HEADS = 32
SSD_HEAD_DIM = 64
SSD_GROUPS = 4
SSD_STATE = 128
SSD_CONV = 4
FFN_CONV = 3

LANES = 128
VMEM_LIMIT_BYTES = 56 * 1024 * 1024

ATT_KEYS = 256
SSD_TILE = 128
ROW_TILE = 512
MM_ROWS = 1024
MM_COLS = 1024
FFN_SPLIT = 2
HALO = 16


def _cparams(*sem):
    return pltpu.CompilerParams(dimension_semantics=sem, vmem_limit_bytes=VMEM_LIMIT_BYTES)


def _sigmoid(x):
    return 1.0 / (1.0 + jnp.exp(-x))


def _silu(x):
    return x * _sigmoid(x)


def _rmsnorm(x, w):
    ms = jnp.mean(x * x, axis=-1, keepdims=True)
    return x * lax.rsqrt(ms + EPS) * w


def _split3(x):
    hi = x.astype(BF16)
    r1 = x - hi.astype(F32)
    mid = r1.astype(BF16)
    lo = (r1 - mid.astype(F32)).astype(BF16)
    return hi, mid, lo


def _norm_dt_kernel(x_ref, w_ref, wdt_ref, h_ref, dt_ref):
    h = _rmsnorm(x_ref[...], w_ref[...]).astype(BF16)
    h_ref[...] = h
    dt_ref[...] = jnp.dot(h, wdt_ref[...], preferred_element_type=F32)


def _norm_dt(x, w, wdt, tm):
    t, d = x.shape
    return pl.pallas_call(
        _norm_dt_kernel,
        grid=(t // tm,),
        in_specs=[pl.BlockSpec((tm, d), lambda i: (i, 0)),
                  pl.BlockSpec((1, d), lambda i: (0, 0)),
                  pl.BlockSpec((d, LANES), lambda i: (0, 0))],
        out_specs=[pl.BlockSpec((tm, d), lambda i: (i, 0)),
                   pl.BlockSpec((tm, LANES), lambda i: (i, 0))],
        out_shape=[jax.ShapeDtypeStruct((t, d), BF16),
                   jax.ShapeDtypeStruct((t, LANES), F32)],
        compiler_params=_cparams("parallel"),
        name="norm_dt",
    )(x, w, wdt)


def _mm_kernel(a_ref, b_ref, o_ref):
    o_ref[...] = jnp.dot(a_ref[...], b_ref[...], preferred_element_type=F32).astype(o_ref.dtype)


def _matmul(a, b, tm, tn, name):
    m, k = a.shape
    n = b.shape[1]
    return pl.pallas_call(
        _mm_kernel,
        grid=(m // tm, n // tn),
        in_specs=[pl.BlockSpec((tm, k), lambda i, j: (i, 0)),
                  pl.BlockSpec((k, tn), lambda i, j: (0, j))],
        out_specs=pl.BlockSpec((tm, tn), lambda i, j: (i, j)),
        out_shape=jax.ShapeDtypeStruct((m, n), BF16),
        compiler_params=_cparams("parallel", "parallel"),
        name=name,
    )(a, b)


def _proj_t_kernel(wt_ref, h_ref, o_ref):
    o_ref[0, 0] = lax.dot_general(wt_ref[...], h_ref[...], (((1,), (1,)), ((), ())),
                                  preferred_element_type=F32).astype(BF16)


def _proj_t(wt, h, bsz, seq, tq):
    n, d = wt.shape
    nq = seq // tq
    return pl.pallas_call(
        _proj_t_kernel,
        grid=(bsz, nq),
        in_specs=[pl.BlockSpec((n, d), lambda b, i: (0, 0)),
                  pl.BlockSpec((tq, d), lambda b, i: (b * nq + i, 0))],
        out_specs=pl.BlockSpec((1, 1, n, tq), lambda b, i: (b, i, 0, 0)),
        out_shape=jax.ShapeDtypeStruct((bsz, nq, n, tq), BF16),
        compiler_params=_cparams("parallel", "parallel"),
        name="proj_t",
    )(wt, h)


def _attn_kernel(slopes_ref, lamv_ref, subw_ref, qt_ref, k_ref, vt_ref, o_ref, rhs_ref, kaug_ref,
                 s_ref, p_ref, acc_ref, stat_ref, *, tk, lam_init):
    tq = 2 * tk
    hd = pl.program_id(1)
    qi = pl.program_id(2)
    a = slopes_ref[hd]

    qt = jnp.concatenate([qt_ref[0, 0], qt_ref[0, 1]], axis=1)
    row = lax.broadcasted_iota(jnp.int32, (ATT_V_DIM, tq), 0)
    zero = jnp.zeros_like(qt)
    ridx = lax.broadcasted_iota(jnp.int32, (8, tq), 0)
    rpos = lax.broadcasted_iota(jnp.int32, (8, tq), 1).astype(F32)
    val = jnp.where(ridx < 3, a, -a * rpos)
    hi = val.astype(BF16).astype(F32)
    r1 = val - hi
    mid = r1.astype(BF16).astype(F32)
    lo = r1 - mid
    sel = ridx % 3
    piece = jnp.where(sel == 0, hi, jnp.where(sel == 1, mid, lo))
    piece = jnp.where(ridx < 6, piece, 0.0)
    aug = jnp.concatenate([piece, jnp.zeros((LANES - 8, tq), F32)], axis=0).astype(BF16)
    rhs_ref[0] = jnp.concatenate([jnp.where(row < ATT_HEAD_DIM, qt, zero), aug], axis=0)
    rhs_ref[1] = jnp.concatenate([jnp.where(row >= ATT_HEAD_DIM, qt, zero), aug], axis=0)

    cpos = lax.broadcasted_iota(jnp.int32, (tk, LANES), 0).astype(F32)
    lane = lax.broadcasted_iota(jnp.int32, (tk, LANES), 1)
    kaug_ref[...] = jnp.where(lane < 3, cpos, jnp.where(lane < 6, 1.0, 0.0)).astype(BF16)

    mx_i, al_i, m_i, l_i = 0, 4, 8, 10
    acc_ref[...] = jnp.zeros_like(acc_ref)
    for mi in range(2):
        stat_ref[m_i + mi] = jnp.full((8, tq), NEG_BIG, F32)
        stat_ref[l_i + mi] = jnp.zeros((8, tq), F32)

    def scores(kt, buf, extra=None):
        start = pl.multiple_of(kt * tk, tk)
        kf = jnp.concatenate([k_ref[0, pl.ds(start, tk), :], kaug_ref[...]], axis=1)
        for mi in range(2):
            s = jnp.dot(kf, rhs_ref[mi], preferred_element_type=F32)
            if extra is not None:
                s = s + extra
            s_ref[2 * mi + buf] = s
            stat_ref[mx_i + 2 * mi + buf, 0:1, :] = jnp.max(s, axis=0, keepdims=True)

    def softmax(kt, buf):
        cj = -a * (qi * tq - kt * tk).astype(F32)
        for mi in range(2):
            sb = 2 * mi + buf
            m_old = stat_ref[m_i + mi, 0:1, :]
            m_new = jnp.maximum(m_old, stat_ref[mx_i + sb, 0:1, :] + cj)
            alpha = jnp.exp2(m_old - m_new)
            p = jnp.exp2(s_ref[sb] - (m_new - cj))
            stat_ref[l_i + mi, 0:1, :] = (alpha * stat_ref[l_i + mi, 0:1, :]
                                          + jnp.sum(p, axis=0, keepdims=True))
            stat_ref[m_i + mi, 0:1, :] = m_new
            stat_ref[al_i + sb, 0:1, :] = alpha
            p_ref[sb] = p.astype(BF16)

    def values(kt, buf):
        vt = vt_ref[0, kt]
        for mi in range(2):
            sb = 2 * mi + buf
            acc_ref[mi] = (stat_ref[al_i + sb, 0:1, :] * acc_ref[mi]
                           + jnp.dot(vt, p_ref[sb], preferred_element_type=F32))

    ci = lax.broadcasted_iota(jnp.int32, (tk, tk), 0)
    ri = lax.broadcasted_iota(jnp.int32, (tk, tk), 1)
    allowed = (ci // ATT_CHUNK) <= (ri // ATT_CHUNK)
    fix = jnp.where(ci > ri, (2.0 * a) * (ri - ci).astype(F32), 0.0)
    dextra = jnp.where(allowed, fix, NEG_BIG)
    d0 = 2 * qi
    scores(d0, 0, jnp.concatenate([dextra, jnp.zeros((tk, tk), F32)], axis=1))
    scores(d0 + 1, 1, jnp.concatenate([jnp.full((tk, tk), NEG_BIG, F32), dextra], axis=1))
    softmax(d0, 0)

    def body(jj, carry):
        first = jj == 0
        older = jnp.where(first, d0, 2 * jj - 2)
        newer = jnp.where(first, d0 + 1, 2 * jj - 1)
        values(older, 0)
        scores(2 * jj, 0)
        softmax(newer, 1)
        values(newer, 1)
        scores(2 * jj + 1, 1)
        softmax(2 * jj, 0)
        return carry

    lax.fori_loop(0, qi, body, 0)
    none = qi == 0
    older = jnp.where(none, d0, d0 - 2)
    newer = jnp.where(none, d0 + 1, d0 - 1)
    values(older, 0)
    softmax(newer, 1)
    values(newer, 1)

    lv = lamv_ref[...]
    t1 = jnp.sum(lv[0:1] * lv[1:2], axis=1, keepdims=True)
    t2 = jnp.sum(lv[2:3] * lv[3:4], axis=1, keepdims=True)
    lam = jnp.exp(t1) - jnp.exp(t2) + lam_init
    o = (acc_ref[0] * (1.0 / stat_ref[l_i, 0:1, :])
         - lam * (acc_ref[1] * (1.0 / stat_ref[l_i + 1, 0:1, :])))
    o = o * lax.rsqrt(jnp.mean(o * o, axis=0, keepdims=True) + EPS)
    o_ref[0] = (o.T * (subw_ref[...] * (1.0 - lam_init))).astype(BF16)


def _attention(slopes, lamv, subw, qvt, k, bsz, seq, tk, lam_init):
    tq = 2 * tk
    nq = seq // tq
    nk = seq // tk
    d = ATT_HEADS * ATT_V_DIM
    return pl.pallas_call(
        functools.partial(_attn_kernel, tk=tk, lam_init=lam_init),
        grid=(bsz, ATT_HEADS, nq),
        in_specs=[pl.BlockSpec(memory_space=pltpu.SMEM),
                  pl.BlockSpec((4, ATT_HEAD_DIM), lambda b, h, i: (0, 0)),
                  pl.BlockSpec((1, ATT_V_DIM), lambda b, h, i: (0, 0)),
                  pl.BlockSpec((1, 2, ATT_V_DIM, tk), lambda b, h, i: (b, i, h, 0)),
                  pl.BlockSpec((1, seq, ATT_V_DIM), lambda b, h, i: (b, 0, h)),
                  pl.BlockSpec((1, nk, ATT_V_DIM, tk), lambda b, h, i: (b, 0, ATT_HEADS + h, 0))],
        out_specs=pl.BlockSpec((1, tq, ATT_V_DIM), lambda b, h, i: (b, i, h)),
        out_shape=jax.ShapeDtypeStruct((bsz, seq, d), BF16),
        scratch_shapes=[pltpu.VMEM((2, 2 * LANES, tq), BF16),
                        pltpu.VMEM((tk, LANES), BF16),
                        pltpu.VMEM((4, tk, tq), F32),
                        pltpu.VMEM((4, tk, tq), BF16),
                        pltpu.VMEM((2, ATT_V_DIM, tq), F32),
                        pltpu.VMEM((12, 8, tq), F32)],
        compiler_params=_cparams("parallel", "parallel", "arbitrary"),
        name="attn",
    )(slopes, lamv, subw, qvt, k, qvt)


def _ssd_kernel(z_ref, xbc_ref, dt_ref, cw_ref, cb_ref, dtb_ref, alog_ref, dexp_ref, nw_ref,
                y_ref, h_scr, halo_scr, *, lc):
    inner = SSD_HEADS * SSD_HEAD_DIM
    gw = inner // SSD_GROUPS
    bc0 = inner
    cc0 = inner + SSD_GROUPS * SSD_STATE

    @pl.when(pl.program_id(1) == 0)
    def _():
        h_scr[...] = jnp.zeros_like(h_scr)
        halo_scr[...] = jnp.zeros_like(halo_scr)

    xraw = xbc_ref[0].astype(F32)
    xext = jnp.concatenate([halo_scr[...], xraw], axis=0)
    halo_scr[...] = xraw[lc - 8:, :]
    cw = cw_ref[...]
    conv = cb_ref[...]
    for kk in range(SSD_CONV):
        off = 8 - (SSD_CONV - 1) + kk
        conv = conv + cw[kk:kk + 1] * xext[off:off + lc]
    xc = _silu(conv)
    xs = xc[:, :inner]

    x = dt_ref[0] + dtb_ref[...]
    dt = jnp.maximum(x, 0.0) + jnp.log(1.0 + jnp.exp(-jnp.abs(x)))
    a = dt * (-jnp.exp(alog_ref[...]))

    ii = lax.broadcasted_iota(jnp.int32, (lc, lc), 0)
    jj = lax.broadcasted_iota(jnp.int32, (lc, lc), 1)
    tril = jj <= ii
    tri = jnp.where(tril, 1.0, 0.0).astype(BF16)
    a3 = jnp.concatenate(_split3(a), axis=1)
    c3 = jnp.dot(tri, a3, preferred_element_type=F32)
    acum = c3[:, :LANES] + c3[:, LANES:2 * LANES] + c3[:, 2 * LANES:]

    er = lax.broadcasted_iota(jnp.int32, (LANES, inner), 0)
    ec = lax.broadcasted_iota(jnp.int32, (LANES, inner), 1)
    expand = jnp.where(ec // SSD_HEAD_DIM == er, 1.0, 0.0).astype(BF16)
    dt_hi = dt.astype(BF16)
    dt_lo = (dt - dt_hi.astype(F32)).astype(BF16)
    stack = jnp.concatenate(list(_split3(acum)) + [dt_hi, dt_lo], axis=0)
    ex = jnp.dot(stack, expand, preferred_element_type=F32)
    acum_e = ex[:lc] + ex[lc:2 * lc] + ex[2 * lc:3 * lc]
    dt_e = ex[3 * lc:4 * lc] + ex[4 * lc:]

    acum_last = acum_e[lc - 1:lc, :]
    xd = xs * dt_e
    xd_b = xd.astype(BF16)
    xd_end = (xd * jnp.exp(acum_last - acum_e)).astype(BF16)
    dec_in = jnp.exp(acum_e)
    dec_all = jnp.exp(acum_last)
    acum_t = acum.T

    lane = lax.broadcasted_iota(jnp.int32, (lc, LANES), 1)
    y_parts = []
    for g in range(SSD_GROUPS):
        bg = xc[:, bc0 + g * SSD_STATE:bc0 + (g + 1) * SSD_STATE]
        cg = xc[:, cc0 + g * SSD_STATE:cc0 + (g + 1) * SSD_STATE].astype(BF16)
        gsl = slice(g * gw, (g + 1) * gw)
        cbt = lax.dot_general(cg, bg.astype(BF16), (((1,), (1,)), ((), ())),
                              preferred_element_type=F32)
        h_prev = h_scr[:, gsl]
        y_off = jnp.dot(cg, h_prev.astype(BF16), preferred_element_type=F32) * dec_in[:, gsl]
        st = jnp.dot(bg.T.astype(BF16), xd_end[:, gsl], preferred_element_type=F32)
        h_scr[:, gsl] = h_prev * dec_all[:, gsl] + st
        heads_per_group = SSD_HEADS // SSD_GROUPS
        for pr in range(heads_per_group // 2):
            r0 = g * heads_per_group + 2 * pr
            ms = []
            for r in (r0, r0 + 1):
                seg = acum[:, r:r + 1] - acum_t[r:r + 1, :]
                ms.append((cbt * jnp.exp(jnp.where(tril, seg, NEG_BIG))).astype(BF16))
            xp = xd_b[:, r0 * SSD_HEAD_DIM:r0 * SSD_HEAD_DIM + LANES]
            zero = jnp.zeros_like(xp)
            xblk = jnp.concatenate([jnp.where(lane < SSD_HEAD_DIM, xp, zero),
                                    jnp.where(lane >= SSD_HEAD_DIM, xp, zero)], axis=0)
            y_diag = jnp.dot(jnp.concatenate(ms, axis=1), xblk, preferred_element_type=F32)
            lsl = slice(pr * LANES, (pr + 1) * LANES)
            y_parts.append(y_diag + y_off[:, lsl])
    y = jnp.concatenate(y_parts, axis=1) + xs * dexp_ref[...]

    gated = y * _silu(z_ref[0].astype(F32))
    outs = []
    for g in range(SSD_GROUPS):
        gg = gated[:, g * gw:(g + 1) * gw]
        outs.append(gg * lax.rsqrt(jnp.mean(gg * gg, axis=-1, keepdims=True) + EPS))
    y_ref[0] = (jnp.concatenate(outs, axis=1) * nw_ref[...]).astype(BF16)


def _ssd(z, xbc, dt, cw, cb, dtb, alog, dexp, nw, lc):
    bsz, seq, inner = z.shape
    cdim = xbc.shape[-1]
    full = lambda b, c: (0, 0)
    return pl.pallas_call(
        functools.partial(_ssd_kernel, lc=lc),
        grid=(bsz, seq // lc),
        in_specs=[pl.BlockSpec((1, lc, inner), lambda b, c: (b, c, 0)),
                  pl.BlockSpec((1, lc, cdim), lambda b, c: (b, c, 0)),
                  pl.BlockSpec((1, lc, LANES), lambda b, c: (b, c, 0)),
                  pl.BlockSpec((SSD_CONV, cdim), full),
                  pl.BlockSpec((1, cdim), full),
                  pl.BlockSpec((1, LANES), full),
                  pl.BlockSpec((1, LANES), full),
                  pl.BlockSpec((1, inner), full),
                  pl.BlockSpec((1, inner), full)],
        out_specs=pl.BlockSpec((1, lc, inner), lambda b, c: (b, c, 0)),
        out_shape=jax.ShapeDtypeStruct((bsz, seq, inner), BF16),
        scratch_shapes=[pltpu.VMEM((SSD_STATE, inner), F32), pltpu.VMEM((8, cdim), F32)],
        compiler_params=_cparams("parallel", "arbitrary"),
        name="ssd",
    )(z, xbc, dt, cw, cb, dtb, alog, dexp, nw)


def _merge_kernel(x_ref, ya_ref, ys_ref, gp_ref, gb_ref, wa_ref, ws_ref, wo_ref, o_ref):
    d = x_ref.shape[-1]
    gates = _sigmoid(gp_ref[...].astype(F32) + gb_ref[...])
    pa = jnp.dot(ya_ref[...], wa_ref[...], preferred_element_type=F32)
    ps = jnp.dot(ys_ref[...], ws_ref[...], preferred_element_type=F32)
    merged = gates[:, :d] * pa + gates[:, d:] * ps
    o_ref[...] = x_ref[...] + jnp.dot(merged.astype(BF16), wo_ref[...], preferred_element_type=F32)


def _merge(x, ya, ys, gp, gb, wa, ws, wo, tm):
    t, d = x.shape
    row = lambda i: (i, 0)
    full = lambda i: (0, 0)
    return pl.pallas_call(
        _merge_kernel,
        grid=(t // tm,),
        in_specs=[pl.BlockSpec((tm, d), row),
                  pl.BlockSpec((tm, ya.shape[1]), row),
                  pl.BlockSpec((tm, ys.shape[1]), row),
                  pl.BlockSpec((tm, gp.shape[1]), row),
                  pl.BlockSpec(gb.shape, full),
                  pl.BlockSpec(wa.shape, full),
                  pl.BlockSpec(ws.shape, full),
                  pl.BlockSpec(wo.shape, full)],
        out_specs=pl.BlockSpec((tm, d), row),
        out_shape=jax.ShapeDtypeStruct((t, d), F32),
        compiler_params=_cparams("parallel"),
        name="merge",
    )(x, ya, ys, gp, gb, wa, ws, wo)


def _ffn_kernel(res_ref, halo_ref, nw_ref, wv_ref, wg_ref, cwv_ref, cwg_ref, cbv_ref, cbg_ref,
                wd_ref, fnw_ref, o_ref, h_scr, acc_scr, *, tm, tiles_per_seq, final_norm):
    i = pl.program_id(0)
    j = pl.program_id(1)

    @pl.when(j == 0)
    def _():
        nw = nw_ref[...]
        hh = _rmsnorm(halo_ref[...], nw)
        hh = jnp.where(i % tiles_per_seq == 0, 0.0, hh)
        h_scr[0:HALO, :] = hh.astype(BF16)
        h_scr[HALO:, :] = _rmsnorm(res_ref[...], nw).astype(BF16)

    hx = h_scr[...]

    def branch(w_ref, cw_ref, cb_ref):
        u = jnp.dot(hx, w_ref[...], preferred_element_type=F32)
        cw = cw_ref[...]
        out = cb_ref[...]
        for kk in range(FFN_CONV):
            off = HALO - (FFN_CONV - 1) + kk
            out = out + cw[kk:kk + 1] * u[off:off + tm]
        return out

    act = (_silu(branch(wg_ref, cwg_ref, cbg_ref)) * branch(wv_ref, cwv_ref, cbv_ref)).astype(BF16)
    part = jnp.dot(act, wd_ref[...], preferred_element_type=F32)

    @pl.when(j == 0)
    def _():
        acc_scr[...] = part

    @pl.when(j > 0)
    def _():
        acc_scr[...] += part

    @pl.when(j == pl.num_programs(1) - 1)
    def _():
        r = res_ref[...] + acc_scr[...]
        o_ref[...] = _rmsnorm(r, fnw_ref[...]) if final_norm else r


def _ffn(res, nw, w_up, cw, cb, w_down, fnw, seq, tm, final_norm):
    t, d = res.shape
    hid = w_down.shape[0]
    th = hid // FFN_SPLIT
    nsplit = FFN_SPLIT
    row = lambda i, j: (i, 0)
    full = lambda i, j: (0, 0)
    halo_blocks = tm // HALO
    return pl.pallas_call(
        functools.partial(_ffn_kernel, tm=tm, tiles_per_seq=seq // tm, final_norm=final_norm),
        grid=(t // tm, nsplit),
        in_specs=[pl.BlockSpec((tm, d), row),
                  pl.BlockSpec((HALO, d), lambda i, j: (jnp.maximum(i * halo_blocks - 1, 0), 0)),
                  pl.BlockSpec((1, d), full),
                  pl.BlockSpec((d, th), lambda i, j: (0, j)),
                  pl.BlockSpec((d, th), lambda i, j: (0, nsplit + j)),
                  pl.BlockSpec((FFN_CONV, th), lambda i, j: (0, j)),
                  pl.BlockSpec((FFN_CONV, th), lambda i, j: (0, nsplit + j)),
                  pl.BlockSpec((1, th), lambda i, j: (0, j)),
                  pl.BlockSpec((1, th), lambda i, j: (0, nsplit + j)),
                  pl.BlockSpec((th, d), lambda i, j: (j, 0)),
                  pl.BlockSpec((1, d), full)],
        out_specs=pl.BlockSpec((tm, d), row),
        out_shape=jax.ShapeDtypeStruct((t, d), F32),
        scratch_shapes=[pltpu.VMEM((HALO + tm, d), BF16), pltpu.VMEM((tm, d), F32)],
        compiler_params=_cparams("parallel", "arbitrary"),
        name="ffn",
    )(res, res, nw, w_up, w_up, cw, cw, cb, cb, w_down, fnw)


def kernel(x, mix_norm_w, w_in, gate_b, lambda_q1, lambda_k1, lambda_q2, lambda_k2, attn_subln_w,
           ssd_conv_w, ssd_conv_b, ssd_dt_bias, ssd_a_log, ssd_d, ssd_norm_w, w_attn_branch,
           w_ssd_branch, w_out, ffn_norm_w, w_up, ffn_conv_w, ffn_conv_b, w_down, final_norm_w):
    bsz, seq, d = x.shape
    t = bsz * seq
    depth = w_in.shape[0]
    att_w = ATT_HEADS * ATT_V_DIM
    inner = SSD_HEADS * SSD_HEAD_DIM
    cdim = inner + 2 * SSD_GROUPS * SSD_STATE
    c_q, c_k, c_v, c_z, c_x, c_dt = (att_w, 2 * att_w, 3 * att_w, 3 * att_w + inner,
                                     3 * att_w + inner + cdim, 3 * att_w + inner + cdim + SSD_HEADS)
    tk = min(ATT_KEYS, seq // 2)
    lc = min(SSD_TILE, seq)
    tm = min(ROW_TILE, seq)
    mm_rows = min(MM_ROWS, t)
    slopes = (jnp.exp2(-8.0 * jnp.arange(1, ATT_HEADS + 1, dtype=F32) / ATT_HEADS) * LOG2E).astype(F32)
    pad_heads = lambda v: jnp.pad(v.astype(F32), (0, LANES - SSD_HEADS)).reshape(1, LANES)

    res = x.reshape(t, d)
    for l in range(depth):
        w = w_in[l]
        lam_init = 0.8 - 0.6 * math.exp(-0.3 * l)
        wq = w[:, :c_q] * (ATT_HEAD_DIM ** -0.5 * LOG2E)
        wqv_t = jnp.concatenate([wq, w[:, c_k:c_v]], axis=1).T.astype(BF16)
        wdt = jnp.pad(w[:, c_x:c_dt], ((0, 0), (0, LANES - SSD_HEADS))).astype(BF16)

        h, dt_raw = _norm_dt(res, mix_norm_w[l].reshape(1, d), wdt, tm)
        k = _matmul(h, w[:, c_q:c_k].astype(BF16), mm_rows, MM_COLS, "proj_k")
        z = _matmul(h, w[:, c_v:c_z].astype(BF16), mm_rows, MM_COLS, "proj_z")
        xbc = _matmul(h, w[:, c_z:c_x].astype(BF16), mm_rows, MM_COLS, "proj_xbc")
        gate_pre = _matmul(h, w[:, c_dt:].astype(BF16), mm_rows, MM_COLS, "proj_gate")
        qvt = _proj_t(wqv_t, h, bsz, seq, tk)

        lamv = jnp.stack([lambda_q1[l], lambda_k1[l], lambda_q2[l], lambda_k2[l]]).astype(F32)
        y_att = _attention(slopes, lamv, attn_subln_w[l].reshape(1, ATT_V_DIM).astype(F32), qvt,
                           k.reshape(bsz, seq, att_w), bsz, seq, tk, lam_init)
        y_ssd = _ssd(z.reshape(bsz, seq, inner), xbc.reshape(bsz, seq, cdim),
                     dt_raw.reshape(bsz, seq, LANES), ssd_conv_w[l].astype(F32),
                     ssd_conv_b[l].reshape(1, cdim).astype(F32), pad_heads(ssd_dt_bias[l]),
                     pad_heads(ssd_a_log[l]),
                     jnp.repeat(ssd_d[l].astype(F32), SSD_HEAD_DIM).reshape(1, inner),
                     ssd_norm_w[l].reshape(1, inner).astype(F32), lc)
        res = _merge(res, y_att.reshape(t, att_w), y_ssd.reshape(t, inner), gate_pre,
                     gate_b[l].reshape(1, -1).astype(F32), w_attn_branch[l].astype(BF16),
                     w_ssd_branch[l].astype(BF16), w_out[l].astype(BF16), tm)
        res = _ffn(res, ffn_norm_w[l].reshape(1, d).astype(F32), w_up[l].astype(BF16),
                   ffn_conv_w[l].astype(F32), ffn_conv_b[l].reshape(1, -1).astype(F32),
                   w_down[l].astype(BF16), final_norm_w.reshape(1, d).astype(F32), seq, tm,
                   l == depth - 1)
    return res.reshape(bsz, seq, d)
```

```python
import functools
import math

import jax
import jax.numpy as jnp
from jax import lax
from jax.experimental import pallas as pl
from jax.experimental.pallas import tpu as pltpu

F32 = jnp.float32
BF16 = jnp.bfloat16

EPS = 1e-6
LOG2E = math.log2(math.e)
NEG_BIG = -1e30

ATT_HEADS = 8
ATT_HEAD_DIM = 64
ATT_V_DIM = 2 * ATT_HEAD_DIM
ATT_CHUNK = 64
SSD_HEADS = 32
SSD_HEAD_DIM = 64
SSD_GROUPS = 4
SSD_STATE = 128
SSD_CONV = 4
FFN_CONV = 3

LANES = 128
VMEM_LIMIT_BYTES = 56 * 1024 * 1024

ATT_KEYS = 256
ATT_HEADS_PER_STEP = 2
SSD_TILE = 128
ROW_TILE = 512
MM_ROWS = 1024
MM_COLS = 1024
FFN_SPLIT = 2
HALO = 16


def _cparams(*sem):
    return pltpu.CompilerParams(dimension_semantics=sem, vmem_limit_bytes=VMEM_LIMIT_BYTES)


def _sigmoid(x):
    return 0.5 * (jnp.tanh(0.5 * x) + 1.0)


def _silu(x):
    h = 0.5 * x
    return h * (jnp.tanh(h) + 1.0)


def _rmsnorm(x, w):
    ms = jnp.mean(x * x, axis=-1, keepdims=True)
    return x * lax.rsqrt(ms + EPS) * w


def _split3(x):
    hi = x.astype(BF16)
    r1 = x - hi.astype(F32)
    mid = r1.astype(BF16)
    lo = (r1 - mid.astype(F32)).astype(BF16)
    return hi, mid, lo


def _norm_dt_kernel(x_ref, w_ref, wdt_ref, h_ref, dt_ref):
    h = _rmsnorm(x_ref[...], w_ref[...]).astype(BF16)
    h_ref[...] = h
    dt_ref[...] = jnp.dot(h, wdt_ref[...], preferred_element_type=F32)


def _norm_dt(x, w, wdt, tm):
    t, d = x.shape
    return pl.pallas_call(
        _norm_dt_kernel,
        grid=(t // tm,),
        in_specs=[pl.BlockSpec((tm, d), lambda i: (i, 0)),
                  pl.BlockSpec((1, d), lambda i: (0, 0)),
                  pl.BlockSpec((d, LANES), lambda i: (0, 0))],
        out_specs=[pl.BlockSpec((tm, d), lambda i: (i, 0)),
                   pl.BlockSpec((tm, LANES), lambda i: (i, 0))],
        out_shape=[jax.ShapeDtypeStruct((t, d), BF16),
                   jax.ShapeDtypeStruct((t, LANES), F32)],
        compiler_params=_cparams("parallel"),
        name="norm_dt",
    )(x, w, wdt)


def _mm_kernel(a_ref, b_ref, o_ref):
    o_ref[...] = jnp.dot(a_ref[...], b_ref[...], preferred_element_type=F32).astype(o_ref.dtype)


def _matmul(a, b, tm, tn, name):
    m, k = a.shape
    n = b.shape[1]
    return pl.pallas_call(
        _mm_kernel,
        grid=(m // tm, n // tn),
        in_specs=[pl.BlockSpec((tm, k), lambda i, j: (i, 0)),
                  pl.BlockSpec((k, tn), lambda i, j: (0, j))],
        out_specs=pl.BlockSpec((tm, tn), lambda i, j: (i, j)),
        out_shape=jax.ShapeDtypeStruct((m, n), BF16),
        compiler_params=_cparams("parallel", "parallel"),
        name=name,
    )(a, b)


def _proj_t_kernel(wt_ref, h_ref, o_ref):
    o_ref[0, 0] = lax.dot_general(wt_ref[...], h_ref[...], (((1,), (1,)), ((), ())),
                                  preferred_element_type=F32).astype(BF16)


def _proj_t(wt, h, bsz, seq, tq):
    n, d = wt.shape
    nq = seq // tq
    return pl.pallas_call(
        _proj_t_kernel,
        grid=(bsz, nq),
        in_specs=[pl.BlockSpec((n, d), lambda b, i: (0, 0)),
                  pl.BlockSpec((tq, d), lambda b, i: (b * nq + i, 0))],
        out_specs=pl.BlockSpec((1, 1, n, tq), lambda b, i: (b, i, 0, 0)),
        out_shape=jax.ShapeDtypeStruct((bsz, nq, n, tq), BF16),
        compiler_params=_cparams("parallel", "parallel"),
        name="proj_t",
    )(wt, h)


def _attn_kernel(slopes_ref, lamv_ref, subw_ref, qt_ref, k_ref, vt_ref, o_ref, rhs_ref, kaug_ref,
                 s_ref, p_ref, acc_ref, stat_ref, *, tk, lam_init):
    nh = ATT_HEADS_PER_STEP
    tq = 2 * tk
    hp = pl.program_id(1)
    qi = pl.program_id(2)
    slope = [slopes_ref[hp * nh + hh] for hh in range(nh)]

    n_stat = 10
    mx_i, al_i, m_i, l_i = 0, 4, 6, 8

    row = lax.broadcasted_iota(jnp.int32, (ATT_V_DIM, tq), 0)
    ridx = lax.broadcasted_iota(jnp.int32, (8, tq), 0)
    rpos = lax.broadcasted_iota(jnp.int32, (8, tq), 1).astype(F32)
    for hh in range(nh):
        a = slope[hh]
        hsl = slice(hh * ATT_V_DIM, (hh + 1) * ATT_V_DIM)
        qt = jnp.concatenate([qt_ref[0, 0, hsl, :], qt_ref[0, 1, hsl, :]], axis=1)
        zero = jnp.zeros_like(qt)
        val = jnp.where(ridx < 3, a, -a * rpos)
        hi = val.astype(BF16).astype(F32)
        r1 = val - hi
        mid = r1.astype(BF16).astype(F32)
        lo = r1 - mid
        sel = ridx % 3
        piece = jnp.where(sel == 0, hi, jnp.where(sel == 1, mid, lo))
        piece = jnp.where(ridx < 6, piece, 0.0)
        aug = jnp.concatenate([piece, jnp.zeros((LANES - 8, tq), F32)], axis=0).astype(BF16)
        rhs_ref[2 * hh] = jnp.concatenate([jnp.where(row < ATT_HEAD_DIM, qt, zero), aug], axis=0)
        rhs_ref[2 * hh + 1] = jnp.concatenate([jnp.where(row >= ATT_HEAD_DIM, qt, zero), aug],
                                              axis=0)
        for mi in range(2):
            base = hh * n_stat
            stat_ref[base + m_i + mi] = jnp.full((8, tq), NEG_BIG, F32)
            stat_ref[base + l_i + mi] = jnp.zeros((8, tq), F32)
            stat_ref[base + al_i + mi] = jnp.ones((8, tq), F32)

    cpos = lax.broadcasted_iota(jnp.int32, (tk, LANES), 0).astype(F32)
    lane = lax.broadcasted_iota(jnp.int32, (tk, LANES), 1)
    kaug_ref[...] = jnp.where(lane < 3, cpos, jnp.where(lane < 6, 1.0, 0.0)).astype(BF16)
    acc_ref[...] = jnp.zeros_like(acc_ref)
    p_ref[...] = jnp.zeros_like(p_ref)

    def scores(kt, diag):
        if diag:
            ci = lax.broadcasted_iota(jnp.int32, (tk, tk), 0)
            ri = lax.broadcasted_iota(jnp.int32, (tk, tk), 1)
            allowed = (ci // ATT_CHUNK) <= (ri // ATT_CHUNK)
            later = ci > ri
            dist2 = 2.0 * (ri - ci).astype(F32)
        for u in range(2):
            start = pl.multiple_of((2 * kt + u) * tk, tk)
            kblk = k_ref[0, pl.ds(start, tk), :]
            for hh in range(nh):
                kf = jnp.concatenate([kblk[:, hh * ATT_V_DIM:(hh + 1) * ATT_V_DIM], kaug_ref[...]],
                                     axis=1)
                if diag:
                    dextra = jnp.where(allowed, jnp.where(later, slope[hh] * dist2, 0.0), NEG_BIG)
                    other = jnp.full((tk, tk), 0.0 if u == 0 else NEG_BIG, F32)
                    extra = jnp.concatenate([dextra, other] if u == 0 else [other, dextra], axis=1)
                for mi in range(2):
                    s = jnp.dot(kf, rhs_ref[2 * hh + mi], preferred_element_type=F32)
                    if diag:
                        s = s + extra
                    s_ref[4 * hh + 2 * mi + u] = s
                    stat_ref[hh * n_stat + mx_i + 2 * mi + u, 0:1, :] = jnp.max(s, axis=0,
                                                                                keepdims=True)

    def softmax(kt):
        for hh in range(nh):
            base = hh * n_stat
            cj = [-slope[hh] * (qi * tq - (2 * kt + u) * tk).astype(F32) for u in range(2)]
            for mi in range(2):
                m_old = stat_ref[base + m_i + mi, 0:1, :]
                mx = jnp.maximum(stat_ref[base + mx_i + 2 * mi, 0:1, :] + cj[0],
                                 stat_ref[base + mx_i + 2 * mi + 1, 0:1, :] + cj[1])
                m_new = jnp.maximum(m_old, mx)
                alpha = jnp.exp2(m_old - m_new)
                lsum = alpha * stat_ref[base + l_i + mi, 0:1, :]
                for u in range(2):
                    p = jnp.exp2(s_ref[4 * hh + 2 * mi + u] - (m_new - cj[u]))
                    lsum = lsum + jnp.sum(p, axis=0, keepdims=True)
                    p_ref[2 * hh + mi, u * tk:(u + 1) * tk, :] = p.astype(BF16)
                stat_ref[base + l_i + mi, 0:1, :] = lsum
                stat_ref[base + m_i + mi, 0:1, :] = m_new
                stat_ref[base + al_i + mi, 0:1, :] = alpha

    def values(kt):
        vblk = jnp.concatenate([vt_ref[0, 2 * kt], vt_ref[0, 2 * kt + 1]], axis=1)
        for hh in range(nh):
            vt = vblk[hh * ATT_V_DIM:(hh + 1) * ATT_V_DIM]
            for mi in range(2):
                acc_ref[2 * hh + mi] = (
                    stat_ref[hh * n_stat + al_i + mi, 0:1, :] * acc_ref[2 * hh + mi]
                    + jnp.dot(vt, p_ref[2 * hh + mi], preferred_element_type=F32))

    scores(qi, True)

    def body(s, carry):
        values(jnp.where(s == 2, qi, jnp.maximum(s - 3, 0)))
        softmax(jnp.where(s == 1, qi, s - 2))
        scores(s - 1, False)
        return carry

    lax.fori_loop(1, qi + 1, body, 0)
    values(jnp.where(qi == 1, qi, jnp.maximum(qi - 2, 0)))
    last = jnp.where(qi == 0, qi, qi - 1)
    softmax(last)
    values(last)

    lv = lamv_ref[...]
    t1 = jnp.sum(lv[0:1] * lv[1:2], axis=1, keepdims=True)
    t2 = jnp.sum(lv[2:3] * lv[3:4], axis=1, keepdims=True)
    lam = jnp.exp(t1) - jnp.exp(t2) + lam_init
    subw = subw_ref[...] * (1.0 - lam_init)
    for hh in range(nh):
        base = hh * n_stat
        o = (acc_ref[2 * hh] * (1.0 / stat_ref[base + l_i, 0:1, :])
             - lam * (acc_ref[2 * hh + 1] * (1.0 / stat_ref[base + l_i + 1, 0:1, :])))
        o = o * lax.rsqrt(jnp.mean(o * o, axis=0, keepdims=True) + EPS)
        o_ref[0, :, hh * ATT_V_DIM:(hh + 1) * ATT_V_DIM] = (o.T * subw).astype(BF16)


def _attention(slopes, lamv, subw, qvt, k, bsz, seq, tk, lam_init):
    nh = ATT_HEADS_PER_STEP
    tq = 2 * tk
    nq = seq // tq
    nk = seq // tk
    d = ATT_HEADS * ATT_V_DIM
    hw = nh * ATT_V_DIM
    return pl.pallas_call(
        functools.partial(_attn_kernel, tk=tk, lam_init=lam_init),
        grid=(bsz, ATT_HEADS // nh, nq),
        in_specs=[pl.BlockSpec(memory_space=pltpu.SMEM),
                  pl.BlockSpec((4, ATT_HEAD_DIM), lambda b, h, i: (0, 0)),
                  pl.BlockSpec((1, ATT_V_DIM), lambda b, h, i: (0, 0)),
                  pl.BlockSpec((1, 2, hw, tk), lambda b, h, i: (b, i, h, 0)),
                  pl.BlockSpec((1, seq, hw), lambda b, h, i: (b, 0, h)),
                  pl.BlockSpec((1, nk, hw, tk), lambda b, h, i: (b, 0, ATT_HEADS // nh + h, 0))],
        out_specs=pl.BlockSpec((1, tq, hw), lambda b, h, i: (b, i, h)),
        out_shape=jax.ShapeDtypeStruct((bsz, seq, d), BF16),
        scratch_shapes=[pltpu.VMEM((2 * nh, 2 * LANES, tq), BF16),
                        pltpu.VMEM((tk, LANES), BF16),
                        pltpu.VMEM((4 * nh, tk, tq), F32),
                        pltpu.VMEM((2 * nh, tq, tq), BF16),
                        pltpu.VMEM((2 * nh, ATT_V_DIM, tq), F32),
                        pltpu.VMEM((10 * nh, 8, tq), F32)],
        compiler_params=_cparams("parallel", "parallel", "arbitrary"),
        name="attn",
    )(slopes, lamv, subw, qvt, k, qvt)


def _ssd_kernel(z_ref, xbc_ref, dt_ref, cw_ref, cb_ref, dtb_ref, alog_ref, dexp_ref, nw_ref,
                y_ref, h_scr, xpad_scr, *, lc):
    inner = SSD_HEADS * SSD_HEAD_DIM
    gw = inner // SSD_GROUPS
    bc0 = inner
    cc0 = inner + SSD_GROUPS * SSD_STATE

    @pl.when(pl.program_id(1) == 0)
    def _():
        h_scr[...] = jnp.zeros_like(h_scr)
        xpad_scr[0:HALO, :] = jnp.zeros((HALO, xpad_scr.shape[1]), BF16)

    xraw = xbc_ref[0]
    xpad_scr[HALO:, :] = xraw
    xpad = xpad_scr[...]
    cw = cw_ref[...]
    conv = cb_ref[...] + cw[SSD_CONV - 1:SSD_CONV] * xraw.astype(F32)
    srow = lax.broadcasted_iota(jnp.int32, (lc, HALO + lc), 0)
    scol = lax.broadcasted_iota(jnp.int32, (lc, HALO + lc), 1)
    for kk in range(SSD_CONV - 1):
        shift = jnp.where(scol == srow + (HALO - (SSD_CONV - 1) + kk), 1.0, 0.0).astype(BF16)
        conv = conv + cw[kk:kk + 1] * jnp.dot(shift, xpad, preferred_element_type=F32)
    xpad_scr[0:HALO, :] = xraw[lc - HALO:, :]
    xc = _silu(conv)
    xs = xc[:, :inner]

    x = dt_ref[0] + dtb_ref[...]
    dt = jnp.maximum(x, 0.0) + jnp.log(1.0 + jnp.exp(-jnp.abs(x)))
    a = dt * (-jnp.exp(alog_ref[...]))

    ii = lax.broadcasted_iota(jnp.int32, (lc, lc), 0)
    jj = lax.broadcasted_iota(jnp.int32, (lc, lc), 1)
    tril = jj <= ii
    tri = jnp.where(tril, 1.0, 0.0).astype(BF16)
    a3 = jnp.concatenate(_split3(a), axis=1)
    c3 = jnp.dot(tri, a3, preferred_element_type=F32)
    acum = c3[:, :LANES] + c3[:, LANES:2 * LANES] + c3[:, 2 * LANES:]

    er = lax.broadcasted_iota(jnp.int32, (LANES, inner), 0)
    ec = lax.broadcasted_iota(jnp.int32, (LANES, inner), 1)
    expand = jnp.where(ec // SSD_HEAD_DIM == er, 1.0, 0.0).astype(BF16)
    dt_hi = dt.astype(BF16)
    dt_lo = (dt - dt_hi.astype(F32)).astype(BF16)
    stack = jnp.concatenate(list(_split3(acum)) + [dt_hi, dt_lo], axis=0)
    ex = jnp.dot(stack, expand, preferred_element_type=F32)
    acum_e = ex[:lc] + ex[lc:2 * lc] + ex[2 * lc:3 * lc]
    dt_e = ex[3 * lc:4 * lc] + ex[4 * lc:]

    acum_last = acum_e[lc - 1:lc, :]
    xd = xs * dt_e
    xd_b = xd.astype(BF16)
    xd_end = (xd * jnp.exp(acum_last - acum_e)).astype(BF16)
    dec_in = jnp.exp(acum_e)
    dec_all = jnp.exp(acum_last)
    acum_t = acum.T

    lane = lax.broadcasted_iota(jnp.int32, (lc, LANES), 1)
    y_parts = []
    for g in range(SSD_GROUPS):
        bg = xc[:, bc0 + g * SSD_STATE:bc0 + (g + 1) * SSD_STATE]
        cg = xc[:, cc0 + g * SSD_STATE:cc0 + (g + 1) * SSD_STATE].astype(BF16)
        gsl = slice(g * gw, (g + 1) * gw)
        cbt = lax.dot_general(cg, bg.astype(BF16), (((1,), (1,)), ((), ())),
                              preferred_element_type=F32)
        h_prev = h_scr[:, gsl]
        y_off = jnp.dot(cg, h_prev.astype(BF16), preferred_element_type=F32) * dec_in[:, gsl]
        st = jnp.dot(bg.T.astype(BF16), xd_end[:, gsl], preferred_element_type=F32)
        h_scr[:, gsl] = h_prev * dec_all[:, gsl] + st
        heads_per_group = SSD_HEADS // SSD_GROUPS
        for pr in range(heads_per_group // 2):
            r0 = g * heads_per_group + 2 * pr
            ms = []
            for r in (r0, r0 + 1):
                seg = acum[:, r:r + 1] - acum_t[r:r + 1, :]
                ms.append((cbt * jnp.exp(jnp.where(tril, seg, NEG_BIG))).astype(BF16))
            xp = xd_b[:, r0 * SSD_HEAD_DIM:r0 * SSD_HEAD_DIM + LANES]
            zero = jnp.zeros_like(xp)
            xblk = jnp.concatenate([jnp.where(lane < SSD_HEAD_DIM, xp, zero),
                                    jnp.where(lane >= SSD_HEAD_DIM, xp, zero)], axis=0)
            y_diag = jnp.dot(jnp.concatenate(ms, axis=1), xblk, preferred_element_type=F32)
            lsl = slice(pr * LANES, (pr + 1) * LANES)
            y_parts.append(y_diag + y_off[:, lsl])
    y = jnp.concatenate(y_parts, axis=1) + xs * dexp_ref[...]

    gated = y * _silu(z_ref[0].astype(F32))
    outs = []
    for g in range(SSD_GROUPS):
        gg = gated[:, g * gw:(g + 1) * gw]
        outs.append(gg * lax.rsqrt(jnp.mean(gg * gg, axis=-1, keepdims=True) + EPS))
    y_ref[0] = (jnp.concatenate(outs, axis=1) * nw_ref[...]).astype(BF16)


def _ssd(z, xbc, dt, cw, cb, dtb, alog, dexp, nw, lc):
    bsz, seq, inner = z.shape
    cdim = xbc.shape[-1]
    full = lambda b, c: (0, 0)
    return pl.pallas_call(
        functools.partial(_ssd_kernel, lc=lc),
        grid=(bsz, seq // lc),
        in_specs=[pl.BlockSpec((1, lc, inner), lambda b, c: (b, c, 0)),
                  pl.BlockSpec((1, lc, cdim), lambda b, c: (b, c, 0)),
                  pl.BlockSpec((1, lc, LANES), lambda b, c: (b, c, 0)),
                  pl.BlockSpec((SSD_CONV, cdim), full),
                  pl.BlockSpec((1, cdim), full),
                  pl.BlockSpec((1, LANES), full),
                  pl.BlockSpec((1, LANES), full),
                  pl.BlockSpec((1, inner), full),
                  pl.BlockSpec((1, inner), full)],
        out_specs=pl.BlockSpec((1, lc, inner), lambda b, c: (b, c, 0)),
        out_shape=jax.ShapeDtypeStruct((bsz, seq, inner), BF16),
        scratch_shapes=[pltpu.VMEM((SSD_STATE, inner), F32), pltpu.VMEM((HALO + lc, cdim), BF16)],
        compiler_params=_cparams("parallel", "arbitrary"),
        name="ssd",
    )(z, xbc, dt, cw, cb, dtb, alog, dexp, nw)


def _merge_kernel(x_ref, ya_ref, ys_ref, gp_ref, gb_ref, wa_ref, ws_ref, wo_ref, o_ref):
    d = x_ref.shape[-1]
    gates = _sigmoid(gp_ref[...].astype(F32) + gb_ref[...])
    pa = jnp.dot(ya_ref[...], wa_ref[...], preferred_element_type=F32)
    ps = jnp.dot(ys_ref[...], ws_ref[...], preferred_element_type=F32)
    merged = gates[:, :d] * pa + gates[:, d:] * ps
    o_ref[...] = x_ref[...] + jnp.dot(merged.astype(BF16), wo_ref[...], preferred_element_type=F32)


def _merge(x, ya, ys, gp, gb, wa, ws, wo, tm):
    t, d = x.shape
    row = lambda i: (i, 0)
    full = lambda i: (0, 0)
    return pl.pallas_call(
        _merge_kernel,
        grid=(t // tm,),
        in_specs=[pl.BlockSpec((tm, d), row),
                  pl.BlockSpec((tm, ya.shape[1]), row),
                  pl.BlockSpec((tm, ys.shape[1]), row),
                  pl.BlockSpec((tm, gp.shape[1]), row),
                  pl.BlockSpec(gb.shape, full),
                  pl.BlockSpec(wa.shape, full),
                  pl.BlockSpec(ws.shape, full),
                  pl.BlockSpec(wo.shape, full)],
        out_specs=pl.BlockSpec((tm, d), row),
        out_shape=jax.ShapeDtypeStruct((t, d), F32),
        compiler_params=_cparams("parallel"),
        name="merge",
    )(x, ya, ys, gp, gb, wa, ws, wo)


def _ffn_kernel(res_ref, halo_ref, nw_ref, wv_ref, wg_ref, cwv_ref, cwg_ref, cbv_ref, cbg_ref,
                wd_ref, fnw_ref, o_ref, h_scr, acc_scr, ug_scr, uv_scr, *, tm, tiles_per_seq,
                final_norm):
    i = pl.program_id(0)
    j = pl.program_id(1)

    @pl.when(j == 0)
    def _():
        nw = nw_ref[...]
        hh = _rmsnorm(halo_ref[...], nw)
        hh = jnp.where(i % tiles_per_seq == 0, 0.0, hh)
        h_scr[0:HALO, :] = hh.astype(BF16)
        h_scr[HALO:, :] = _rmsnorm(res_ref[...], nw).astype(BF16)

    hx = h_scr[...]

    def branch(w_ref, cw_ref, cb_ref, u_ref):
        u_ref[...] = jnp.dot(hx, w_ref[...], preferred_element_type=F32)
        cw = cw_ref[...]
        out = cb_ref[...]
        for kk in range(FFN_CONV):
            off = HALO - (FFN_CONV - 1) + kk
            out = out + cw[kk:kk + 1] * u_ref[off:off + tm, :]
        return out

    act = (_silu(branch(wg_ref, cwg_ref, cbg_ref, ug_scr))
           * branch(wv_ref, cwv_ref, cbv_ref, uv_scr)).astype(BF16)
    part = jnp.dot(act, wd_ref[...], preferred_element_type=F32)

    @pl.when(j == 0)
    def _():
        acc_scr[...] = part

    @pl.when(j > 0)
    def _():
        acc_scr[...] += part

    @pl.when(j == pl.num_programs(1) - 1)
    def _():
        r = res_ref[...] + acc_scr[...]
        o_ref[...] = _rmsnorm(r, fnw_ref[...]) if final_norm else r


def _ffn(res, nw, w_up, cw, cb, w_down, fnw, seq, tm, final_norm):
    t, d = res.shape
    hid = w_down.shape[0]
    th = hid // FFN_SPLIT
    nsplit = FFN_SPLIT
    row = lambda i, j: (i, 0)
    full = lambda i, j: (0, 0)
    halo_blocks = tm // HALO
    return pl.pallas_call(
        functools.partial(_ffn_kernel, tm=tm, tiles_per_seq=seq // tm, final_norm=final_norm),
        grid=(t // tm, nsplit),
        in_specs=[pl.BlockSpec((tm, d), row),
                  pl.BlockSpec((HALO, d), lambda i, j: (jnp.maximum(i * halo_blocks - 1, 0), 0)),
                  pl.BlockSpec((1, d), full),
                  pl.BlockSpec((d, th), lambda i, j: (0, j)),
                  pl.BlockSpec((d, th), lambda i, j: (0, nsplit + j)),
                  pl.BlockSpec((FFN_CONV, th), lambda i, j: (0, j)),
                  pl.BlockSpec((FFN_CONV, th), lambda i, j: (0, nsplit + j)),
                  pl.BlockSpec((1, th), lambda i, j: (0, j)),
                  pl.BlockSpec((1, th), lambda i, j: (0, nsplit + j)),
                  pl.BlockSpec((th, d), lambda i, j: (j, 0)),
                  pl.BlockSpec((1, d), full)],
        out_specs=pl.BlockSpec((tm, d), row),
        out_shape=jax.ShapeDtypeStruct((t, d), F32),
        scratch_shapes=[pltpu.VMEM((HALO + tm, d), BF16), pltpu.VMEM((tm, d), F32),
                        pltpu.VMEM((HALO + tm, th), F32), pltpu.VMEM((HALO + tm, th), F32)],
        compiler_params=_cparams("parallel", "arbitrary"),
        name="ffn",
    )(res, res, nw, w_up, w_up, cw, cw, cb, cb, w_down, fnw)


def kernel(x, mix_norm_w, w_in, gate_b, lambda_q1, lambda_k1, lambda_q2, lambda_k2, attn_subln_w,
           ssd_conv_w, ssd_conv_b, ssd_dt_bias, ssd_a_log, ssd_d, ssd_norm_w, w_attn_branch,
           w_ssd_branch, w_out, ffn_norm_w, w_up, ffn_conv_w, ffn_conv_b, w_down, final_norm_w):
    bsz, seq, d = x.shape
    t = bsz * seq
    depth = w_in.shape[0]
    att_w = ATT_HEADS * ATT_V_DIM
    inner = SSD_HEADS * SSD_HEAD_DIM
    cdim = inner + 2 * SSD_GROUPS * SSD_STATE
    c_q, c_k, c_v, c_z, c_x, c_dt = (att_w, 2 * att_w, 3 * att_w, 3 * att_w + inner,
                                     3 * att_w + inner + cdim, 3 * att_w + inner + cdim + SSD_HEADS)
    tk = min(ATT_KEYS, seq // 2)
    lc = min(SSD_TILE, seq)
    tm = min(ROW_TILE, seq)
    mm_rows = min(MM_ROWS, t)
    slopes = (jnp.exp2(-8.0 * jnp.arange(1, ATT_HEADS + 1, dtype=F32) / ATT_HEADS) * LOG2E).astype(F32)
    pad_heads = lambda v: jnp.pad(v.astype(F32), (0, LANES - SSD_HEADS)).reshape(1, LANES)

    res = x.reshape(t, d)
    for l in range(depth):
        w = w_in[l]
        lam_init = 0.8 - 0.6 * math.exp(-0.3 * l)
        wq = w[:, :c_q] * (ATT_HEAD_DIM ** -0.5 * LOG2E)
        wqv_t = jnp.concatenate([wq, w[:, c_k:c_v]], axis=1).T.astype(BF16)
        wdt = jnp.pad(w[:, c_x:c_dt], ((0, 0), (0, LANES - SSD_HEADS))).astype(BF16)

        h, dt_raw = _norm_dt(res, mix_norm_w[l].reshape(1, d), wdt, tm)
        k = _matmul(h, w[:, c_q:c_k].astype(BF16), mm_rows, MM_COLS, "proj_k")
        z = _matmul(h, w[:, c_v:c_z].astype(BF16), mm_rows, MM_COLS, "proj_z")
        xbc = _matmul(h, w[:, c_z:c_x].astype(BF16), mm_rows, MM_COLS, "proj_xbc")
        gate_pre = _matmul(h, w[:, c_dt:].astype(BF16), mm_rows, MM_COLS, "proj_gate")
        qvt = _proj_t(wqv_t, h, bsz, seq, tk)

        lamv = jnp.stack([lambda_q1[l], lambda_k1[l], lambda_q2[l], lambda_k2[l]]).astype(F32)
        y_att = _attention(slopes, lamv, attn_subln_w[l].reshape(1, ATT_V_DIM).astype(F32), qvt,
                           k.reshape(bsz, seq, att_w), bsz, seq, tk, lam_init)
        y_ssd = _ssd(z.reshape(bsz, seq, inner), xbc.reshape(bsz, seq, cdim),
                     dt_raw.reshape(bsz, seq, LANES), ssd_conv_w[l].astype(F32),
                     ssd_conv_b[l].reshape(1, cdim).astype(F32), pad_heads(ssd_dt_bias[l]),
                     pad_heads(ssd_a_log[l]),
                     jnp.repeat(ssd_d[l].astype(F32), SSD_HEAD_DIM).reshape(1, inner),
                     ssd_norm_w[l].reshape(1, inner).astype(F32), lc)
        res = _merge(res, y_att.reshape(t, att_w), y_ssd.reshape(t, inner), gate_pre,
                     gate_b[l].reshape(1, -1).astype(F32), w_attn_branch[l].astype(BF16),
                     w_ssd_branch[l].astype(BF16), w_out[l].astype(BF16), tm)
        res = _ffn(res, ffn_norm_w[l].reshape(1, d).astype(F32), w_up[l].astype(BF16),
                   ffn_conv_w[l].astype(F32), ffn_conv_b[l].reshape(1, -1).astype(F32),
                   w_down[l].astype(BF16), final_norm_w.reshape(1, d).astype(F32), seq, tm,
                   l == depth - 1)
    return res.reshape(bsz, seq, d)
```

```python
import functools
import math

import jax
import jax.numpy as jnp
from jax import lax
from jax.experimental import pallas as pl
from jax.experimental.pallas import tpu as pltpu

F32 = jnp.float32
BF16 = jnp.bfloat16

EPS = 1e-6
LOG2E = math.log2(math.e)
NEG_BIG = -1e30

ATT_HEADS = 8
ATT_HEAD_DIM = 64
ATT_V_DIM = 2 * ATT_HEAD_DIM
ATT_CHUNK = 64
SSD_HEADS = 32
SSD_HEAD_DIM = 64
SSD_GROUPS = 4
SSD_STATE = 128
SSD_CONV = 4
FFN_CONV = 3

LANES = 128
VMEM_LIMIT_BYTES = 56 * 1024 * 1024

ATT_KEYS = 256
ATT_HEADS_PER_STEP = 2
SSD_TILE = 128
ROW_TILE = 512
MM_ROWS = 1024
FFN_SPLIT = 2
HALO = 16


def _cparams(*sem):
    return pltpu.CompilerParams(dimension_semantics=sem, vmem_limit_bytes=VMEM_LIMIT_BYTES)


def _sigmoid(x):
    return 0.5 * (jnp.tanh(0.5 * x) + 1.0)


def _silu(x):
    h = 0.5 * x
    return h * (jnp.tanh(h) + 1.0)


def _rmsnorm(x, w):
    ms = jnp.mean(x * x, axis=-1, keepdims=True)
    return x * lax.rsqrt(ms + EPS) * w


def _split3(x):
    hi = x.astype(BF16)
    r1 = x - hi.astype(F32)
    mid = r1.astype(BF16)
    lo = (r1 - mid.astype(F32)).astype(BF16)
    return hi, mid, lo


_NT = (((1,), (1,)), ((), ()))


def _norm_dt_kernel(x_ref, w_ref, wdt_ref, h_ref, dt_ref):
    h = _rmsnorm(x_ref[...], w_ref[...]).astype(BF16)
    h_ref[...] = h
    dt_ref[...] = lax.dot_general(h, wdt_ref[...], _NT, preferred_element_type=F32)


def _norm_dt(x, w, wdt, tm):
    t, d = x.shape
    return pl.pallas_call(
        _norm_dt_kernel,
        grid=(t // tm,),
        in_specs=[pl.BlockSpec((tm, d), lambda i: (i, 0)),
                  pl.BlockSpec((1, d), lambda i: (0, 0)),
                  pl.BlockSpec((LANES, d), lambda i: (0, 0))],
        out_specs=[pl.BlockSpec((tm, d), lambda i: (i, 0)),
                   pl.BlockSpec((tm, LANES), lambda i: (i, 0))],
        out_shape=[jax.ShapeDtypeStruct((t, d), BF16),
                   jax.ShapeDtypeStruct((t, LANES), F32)],
        compiler_params=_cparams("parallel"),
        name="norm_dt",
    )(x, w, wdt)


def _mm_nt_kernel(a_ref, *refs):
    w_refs, o_ref = refs[:-1], refs[-1]
    a = a_ref[...]
    for j, w_ref in enumerate(w_refs):
        n = w_ref.shape[0]
        o_ref[:, j * n:(j + 1) * n] = lax.dot_general(
            a, w_ref[...], _NT, preferred_element_type=F32).astype(o_ref.dtype)


def _matmul_nt(a, wt, row_blocks, blk, tm, name):
    m, k = a.shape
    n = blk * len(row_blocks)
    return pl.pallas_call(
        _mm_nt_kernel,
        grid=(m // tm,),
        in_specs=[pl.BlockSpec((tm, k), lambda i: (i, 0))]
        + [pl.BlockSpec((blk, k), lambda i, r=r: (r, 0)) for r in row_blocks],
        out_specs=pl.BlockSpec((tm, n), lambda i: (i, 0)),
        out_shape=jax.ShapeDtypeStruct((m, n), BF16),
        compiler_params=_cparams("parallel"),
        name=name,
    )(a, *([wt] * len(row_blocks)))


def _proj_t_kernel(wq_ref, wv_ref, h_ref, o_ref, *, tq, q_scale):
    nq = wq_ref.shape[0]
    for s in range(o_ref.shape[1]):
        hs = h_ref[s * tq:(s + 1) * tq, :]
        qt = lax.dot_general(wq_ref[...], hs, _NT, preferred_element_type=F32)
        o_ref[0, s, 0:nq, :] = (qt * q_scale).astype(BF16)
        o_ref[0, s, nq:, :] = lax.dot_general(wv_ref[...], hs, _NT,
                                              preferred_element_type=F32).astype(BF16)


def _proj_t(wt, q_block, v_block, blk, q_scale, h, bsz, seq, tq, tm):
    d = wt.shape[1]
    nq = seq // tq
    per_step = tm // tq
    steps = seq // tm
    return pl.pallas_call(
        functools.partial(_proj_t_kernel, tq=tq, q_scale=q_scale),
        grid=(bsz, steps),
        in_specs=[pl.BlockSpec((blk, d), lambda b, i: (q_block, 0)),
                  pl.BlockSpec((blk, d), lambda b, i: (v_block, 0)),
                  pl.BlockSpec((tm, d), lambda b, i: (b * steps + i, 0))],
        out_specs=pl.BlockSpec((1, per_step, 2 * blk, tq), lambda b, i: (b, i, 0, 0)),
        out_shape=jax.ShapeDtypeStruct((bsz, nq, 2 * blk, tq), BF16),
        compiler_params=_cparams("parallel", "parallel"),
        name="proj_t",
    )(wt, wt, h)


def _attn_kernel(slopes_ref, lamv_ref, subw_ref, qt_ref, k_ref, vt_ref, o_ref, rhs_ref, kaug_ref,
                 s_ref, p_ref, acc_ref, stat_ref, *, tk, lam_init):
    nh = ATT_HEADS_PER_STEP
    tq = 2 * tk
    hp = pl.program_id(1)
    qi = pl.program_id(2)
    slope = [slopes_ref[hp * nh + hh] for hh in range(nh)]

    mx_i = 0
    head_0 = 4 * nh
    n_stat = 6
    al_i, m_i, l_i = 0, 2, 4

    row = lax.broadcasted_iota(jnp.int32, (ATT_V_DIM, tq), 0)
    ridx = lax.broadcasted_iota(jnp.int32, (8, tq), 0)
    rpos = lax.broadcasted_iota(jnp.int32, (8, tq), 1).astype(F32)
    for hh in range(nh):
        a = slope[hh]
        hsl = slice(hh * ATT_V_DIM, (hh + 1) * ATT_V_DIM)
        qt = jnp.concatenate([qt_ref[0, 0, hsl, :], qt_ref[0, 1, hsl, :]], axis=1)
        zero = jnp.zeros_like(qt)
        val = jnp.where(ridx < 3, a, -a * rpos)
        hi = val.astype(BF16).astype(F32)
        r1 = val - hi
        mid = r1.astype(BF16).astype(F32)
        lo = r1 - mid
        sel = ridx % 3
        piece = jnp.where(sel == 0, hi, jnp.where(sel == 1, mid, lo))
        piece = jnp.where(ridx < 6, piece, 0.0)
        aug = jnp.concatenate([piece, jnp.zeros((LANES - 8, tq), F32)], axis=0).astype(BF16)
        rhs_ref[2 * hh] = jnp.concatenate([jnp.where(row < ATT_HEAD_DIM, qt, zero), aug], axis=0)
        rhs_ref[2 * hh + 1] = jnp.concatenate([jnp.where(row >= ATT_HEAD_DIM, qt, zero), aug],
                                              axis=0)
        for mi in range(2):
            base = head_0 + hh * n_stat
            stat_ref[base + m_i + mi] = jnp.full((8, tq), NEG_BIG, F32)
            stat_ref[base + l_i + mi] = jnp.zeros((8, tq), F32)
            stat_ref[base + al_i + mi] = jnp.ones((8, tq), F32)

    cpos = lax.broadcasted_iota(jnp.int32, (tk, LANES), 0).astype(F32)
    lane = lax.broadcasted_iota(jnp.int32, (tk, LANES), 1)
    kaug_ref[...] = jnp.where(lane < 3, cpos, jnp.where(lane < 6, 1.0, 0.0)).astype(BF16)
    acc_ref[...] = jnp.zeros_like(acc_ref)
    p_ref[...] = jnp.zeros_like(p_ref)

    def scores(kt, diag):
        if diag:
            ci = lax.broadcasted_iota(jnp.int32, (tk, tk), 0)
            ri = lax.broadcasted_iota(jnp.int32, (tk, tk), 1)
            allowed = (ci // ATT_CHUNK) <= (ri // ATT_CHUNK)
            later = ci > ri
            dist2 = 2.0 * (ri - ci).astype(F32)
        for u in range(2):
            start = pl.multiple_of((2 * kt + u) * tk, tk)
            kblk = k_ref[0, pl.ds(start, tk), :]
            for hh in range(nh):
                kf = jnp.concatenate([kblk[:, hh * ATT_V_DIM:(hh + 1) * ATT_V_DIM], kaug_ref[...]],
                                     axis=1)
                if diag:
                    dextra = jnp.where(allowed, jnp.where(later, slope[hh] * dist2, 0.0), NEG_BIG)
                    other = jnp.full((tk, tk), 0.0 if u == 0 else NEG_BIG, F32)
                    extra = jnp.concatenate([dextra, other] if u == 0 else [other, dextra], axis=1)
                for mi in range(2):
                    s = jnp.dot(kf, rhs_ref[2 * hh + mi], preferred_element_type=F32)
                    if diag:
                        s = s + extra
                    sb = 4 * hh + 2 * mi + u
                    s_ref[sb] = s
                    stat_ref[mx_i + sb, 0:1, :] = jnp.max(s, axis=0, keepdims=True)

    def softmax(kt):
        for hh in range(nh):
            base = head_0 + hh * n_stat
            cj = [-slope[hh] * (qi * tq - (2 * kt + u) * tk).astype(F32) for u in range(2)]
            for mi in range(2):
                sb = 4 * hh + 2 * mi
                m_old = stat_ref[base + m_i + mi, 0:1, :]
                mx = jnp.maximum(stat_ref[mx_i + sb, 0:1, :] + cj[0],
                                 stat_ref[mx_i + sb + 1, 0:1, :] + cj[1])
                m_new = jnp.maximum(m_old, mx)
                alpha = jnp.exp2(m_old - m_new)
                lsum = alpha * stat_ref[base + l_i + mi, 0:1, :]
                for u in range(2):
                    p = jnp.exp2(s_ref[sb + u] - (m_new - cj[u]))
                    lsum = lsum + jnp.sum(p, axis=0, keepdims=True)
                    p_ref[2 * hh + mi, u * tk:(u + 1) * tk, :] = p.astype(BF16)
                stat_ref[base + l_i + mi, 0:1, :] = lsum
                stat_ref[base + m_i + mi, 0:1, :] = m_new
                stat_ref[base + al_i + mi, 0:1, :] = alpha

    def values(kt):
        vblk = jnp.concatenate([vt_ref[0, 2 * kt], vt_ref[0, 2 * kt + 1]], axis=1)
        for hh in range(nh):
            vt = vblk[hh * ATT_V_DIM:(hh + 1) * ATT_V_DIM]
            for mi in range(2):
                acc_ref[2 * hh + mi] = (
                    stat_ref[head_0 + hh * n_stat + al_i + mi, 0:1, :] * acc_ref[2 * hh + mi]
                    + jnp.dot(vt, p_ref[2 * hh + mi], preferred_element_type=F32))

    scores(qi, True)

    def body(s, carry):
        values(jnp.where(s == 2, qi, jnp.maximum(s - 3, 0)))
        softmax(jnp.where(s == 1, qi, s - 2))
        scores(s - 1, False)
        return carry

    lax.fori_loop(1, qi + 1, body, 0)
    values(jnp.where(qi == 1, qi, jnp.maximum(qi - 2, 0)))
    last = jnp.where(qi == 0, qi, qi - 1)
    softmax(last)
    values(last)

    lv = lamv_ref[...]
    t1 = jnp.sum(lv[0:1] * lv[1:2], axis=1, keepdims=True)
    t2 = jnp.sum(lv[2:3] * lv[3:4], axis=1, keepdims=True)
    lam = jnp.exp(t1) - jnp.exp(t2) + lam_init
    subw = subw_ref[...] * (1.0 - lam_init)
    for hh in range(nh):
        base = head_0 + hh * n_stat
        o = (acc_ref[2 * hh] * (1.0 / stat_ref[base + l_i, 0:1, :])
             - lam * (acc_ref[2 * hh + 1] * (1.0 / stat_ref[base + l_i + 1, 0:1, :])))
        o = o * lax.rsqrt(jnp.mean(o * o, axis=0, keepdims=True) + EPS)
        o_ref[0, :, hh * ATT_V_DIM:(hh + 1) * ATT_V_DIM] = (o.T * subw).astype(BF16)


def _attention(slopes, lamv, subw, qvt, k, bsz, seq, tk, lam_init):
    nh = ATT_HEADS_PER_STEP
    tq = 2 * tk
    nq = seq // tq
    nk = seq // tk
    d = ATT_HEADS * ATT_V_DIM
    hw = nh * ATT_V_DIM
    return pl.pallas_call(
        functools.partial(_attn_kernel, tk=tk, lam_init=lam_init),
        grid=(bsz, ATT_HEADS // nh, nq),
        in_specs=[pl.BlockSpec(memory_space=pltpu.SMEM),
                  pl.BlockSpec((4, ATT_HEAD_DIM), lambda b, h, i: (0, 0)),
                  pl.BlockSpec((1, ATT_V_DIM), lambda b, h, i: (0, 0)),
                  pl.BlockSpec((1, 2, hw, tk), lambda b, h, i: (b, i, h, 0)),
                  pl.BlockSpec((1, seq, hw), lambda b, h, i: (b, 0, h)),
                  pl.BlockSpec((1, nk, hw, tk), lambda b, h, i: (b, 0, ATT_HEADS // nh + h, 0))],
        out_specs=pl.BlockSpec((1, tq, hw), lambda b, h, i: (b, i, h)),
        out_shape=jax.ShapeDtypeStruct((bsz, seq, d), BF16),
        scratch_shapes=[pltpu.VMEM((2 * nh, 2 * LANES, tq), BF16),
                        pltpu.VMEM((tk, LANES), BF16),
                        pltpu.VMEM((4 * nh, tk, tq), F32),
                        pltpu.VMEM((2 * nh, tq, tq), BF16),
                        pltpu.VMEM((2 * nh, ATT_V_DIM, tq), F32),
                        pltpu.VMEM((10 * nh, 8, tq), F32)],
        compiler_params=_cparams("parallel", "parallel", "arbitrary"),
        name="attn",
    )(slopes, lamv, subw, qvt, k, qvt)


def _ssd_kernel(z_ref, xbc_ref, dt_ref, cw_ref, cb_ref, dtb_ref, alog_ref, dexp_ref, nw_ref,
                y_ref, h_scr, xpad_scr, *, lc):
    inner = SSD_HEADS * SSD_HEAD_DIM
    gw = inner // SSD_GROUPS
    bc0 = inner
    cc0 = inner + SSD_GROUPS * SSD_STATE

    @pl.when(pl.program_id(1) == 0)
    def _():
        h_scr[...] = jnp.zeros_like(h_scr)
        xpad_scr[0:HALO, :] = jnp.zeros((HALO, xpad_scr.shape[1]), BF16)

    xraw = xbc_ref[0]
    xpad_scr[HALO:, :] = xraw
    xpad = xpad_scr[...]
    cw = cw_ref[...]
    conv = cb_ref[...] + cw[SSD_CONV - 1:SSD_CONV] * xraw.astype(F32)
    srow = lax.broadcasted_iota(jnp.int32, (lc, HALO + lc), 0)
    scol = lax.broadcasted_iota(jnp.int32, (lc, HALO + lc), 1)
    for kk in range(SSD_CONV - 1):
        shift = jnp.where(scol == srow + (HALO - (SSD_CONV - 1) + kk), 1.0, 0.0).astype(BF16)
        conv = conv + cw[kk:kk + 1] * jnp.dot(shift, xpad, preferred_element_type=F32)
    xpad_scr[0:HALO, :] = xraw[lc - HALO:, :]
    xc = _silu(conv)
    xs = xc[:, :inner]

    x = dt_ref[0] + dtb_ref[...]
    dt = jnp.maximum(x, 0.0) + jnp.log(1.0 + jnp.exp(-jnp.abs(x)))
    a = dt * (-jnp.exp(alog_ref[...]))

    ii = lax.broadcasted_iota(jnp.int32, (lc, lc), 0)
    jj = lax.broadcasted_iota(jnp.int32, (lc, lc), 1)
    tril = jj <= ii
    tri = jnp.where(tril, 1.0, 0.0).astype(BF16)
    a3 = jnp.concatenate(_split3(a), axis=1)
    c3 = jnp.dot(tri, a3, preferred_element_type=F32)
    acum = c3[:, :LANES] + c3[:, LANES:2 * LANES] + c3[:, 2 * LANES:]

    er = lax.broadcasted_iota(jnp.int32, (LANES, inner), 0)
    ec = lax.broadcasted_iota(jnp.int32, (LANES, inner), 1)
    expand = jnp.where(ec // SSD_HEAD_DIM == er, 1.0, 0.0).astype(BF16)
    dt_hi = dt.astype(BF16)
    dt_lo = (dt - dt_hi.astype(F32)).astype(BF16)
    stack = jnp.concatenate(list(_split3(acum)) + [dt_hi, dt_lo], axis=0)
    ex = jnp.dot(stack, expand, preferred_element_type=F32)
    acum_e = ex[:lc] + ex[lc:2 * lc] + ex[2 * lc:3 * lc]
    dt_e = ex[3 * lc:4 * lc] + ex[4 * lc:]

    acum_last = acum_e[lc - 1:lc, :]
    xd = xs * dt_e
    xd_b = xd.astype(BF16)
    xd_end = (xd * jnp.exp(acum_last - acum_e)).astype(BF16)
    dec_in = jnp.exp(acum_e)
    dec_all = jnp.exp(acum_last)
    acum_t = acum.T

    lane = lax.broadcasted_iota(jnp.int32, (lc, LANES), 1)
    y_parts = []
    for g in range(SSD_GROUPS):
        bg = xc[:, bc0 + g * SSD_STATE:bc0 + (g + 1) * SSD_STATE]
        cg = xc[:, cc0 + g * SSD_STATE:cc0 + (g + 1) * SSD_STATE].astype(BF16)
        gsl = slice(g * gw, (g + 1) * gw)
        cbt = lax.dot_general(cg, bg.astype(BF16), (((1,), (1,)), ((), ())),
                              preferred_element_type=F32)
        h_prev = h_scr[:, gsl]
        y_off = jnp.dot(cg, h_prev.astype(BF16), preferred_element_type=F32) * dec_in[:, gsl]
        st = jnp.dot(bg.T.astype(BF16), xd_end[:, gsl], preferred_element_type=F32)
        h_scr[:, gsl] = h_prev * dec_all[:, gsl] + st
        heads_per_group = SSD_HEADS // SSD_GROUPS
        for pr in range(heads_per_group // 2):
            r0 = g * heads_per_group + 2 * pr
            ms = []
            for r in (r0, r0 + 1):
                seg = acum[:, r:r + 1] - acum_t[r:r + 1, :]
                ms.append((cbt * jnp.exp(jnp.where(tril, seg, NEG_BIG))).astype(BF16))
            xp = xd_b[:, r0 * SSD_HEAD_DIM:r0 * SSD_HEAD_DIM + LANES]
            zero = jnp.zeros_like(xp)
            xblk = jnp.concatenate([jnp.where(lane < SSD_HEAD_DIM, xp, zero),
                                    jnp.where(lane >= SSD_HEAD_DIM, xp, zero)], axis=0)
            y_diag = jnp.dot(jnp.concatenate(ms, axis=1), xblk, preferred_element_type=F32)
            lsl = slice(pr * LANES, (pr + 1) * LANES)
            y_parts.append(y_diag + y_off[:, lsl])
    y = jnp.concatenate(y_parts, axis=1) + xs * dexp_ref[...]

    gated = y * _silu(z_ref[0].astype(F32))
    outs = []
    for g in range(SSD_GROUPS):
        gg = gated[:, g * gw:(g + 1) * gw]
        outs.append(gg * lax.rsqrt(jnp.mean(gg * gg, axis=-1, keepdims=True) + EPS))
    y_ref[0] = (jnp.concatenate(outs, axis=1) * nw_ref[...]).astype(BF16)


def _ssd(z, xbc, dt, cw, cb, dtb, alog, dexp, nw, lc):
    bsz, seq, inner = z.shape
    cdim = xbc.shape[-1]
    full = lambda b, c: (0, 0)
    return pl.pallas_call(
        functools.partial(_ssd_kernel, lc=lc),
        grid=(bsz, seq // lc),
        in_specs=[pl.BlockSpec((1, lc, inner), lambda b, c: (b, c, 0)),
                  pl.BlockSpec((1, lc, cdim), lambda b, c: (b, c, 0)),
                  pl.BlockSpec((1, lc, LANES), lambda b, c: (b, c, 0)),
                  pl.BlockSpec((SSD_CONV, cdim), full),
                  pl.BlockSpec((1, cdim), full),
                  pl.BlockSpec((1, LANES), full),
                  pl.BlockSpec((1, LANES), full),
                  pl.BlockSpec((1, inner), full),
                  pl.BlockSpec((1, inner), full)],
        out_specs=pl.BlockSpec((1, lc, inner), lambda b, c: (b, c, 0)),
        out_shape=jax.ShapeDtypeStruct((bsz, seq, inner), BF16),
        scratch_shapes=[pltpu.VMEM((SSD_STATE, inner), F32), pltpu.VMEM((HALO + lc, cdim), BF16)],
        compiler_params=_cparams("parallel", "arbitrary"),
        name="ssd",
    )(z, xbc, dt, cw, cb, dtb, alog, dexp, nw)


def _merge_kernel(x_ref, ya_ref, ys_ref, gp_ref, gb_ref, wa_ref, ws_ref, wo_ref, o_ref):
    d = x_ref.shape[-1]
    gates = _sigmoid(gp_ref[...].astype(F32) + gb_ref[...])
    pa = jnp.dot(ya_ref[...], wa_ref[...], preferred_element_type=F32)
    ps = jnp.dot(ys_ref[...], ws_ref[...], preferred_element_type=F32)
    merged = gates[:, :d] * pa + gates[:, d:] * ps
    o_ref[...] = x_ref[...] + jnp.dot(merged.astype(BF16), wo_ref[...], preferred_element_type=F32)


def _merge(x, ya, ys, gp, gb, wa, ws, wo, tm):
    t, d = x.shape
    row = lambda i: (i, 0)
    full = lambda i: (0, 0)
    return pl.pallas_call(
        _merge_kernel,
        grid=(t // tm,),
        in_specs=[pl.BlockSpec((tm, d), row),
                  pl.BlockSpec((tm, ya.shape[1]), row),
                  pl.BlockSpec((tm, ys.shape[1]), row),
                  pl.BlockSpec((tm, gp.shape[1]), row),
                  pl.BlockSpec(gb.shape, full),
                  pl.BlockSpec(wa.shape, full),
                  pl.BlockSpec(ws.shape, full),
                  pl.BlockSpec(wo.shape, full)],
        out_specs=pl.BlockSpec((tm, d), row),
        out_shape=jax.ShapeDtypeStruct((t, d), F32),
        compiler_params=_cparams("parallel"),
        name="merge",
    )(x, ya, ys, gp, gb, wa, ws, wo)


def _ffn_kernel(res_ref, halo_ref, nw_ref, wup_ref, cw_ref, cb_ref, wd_ref, fnw_ref, o_ref, h_scr,
                u_scr, *, tm, hid, tiles_per_seq, final_norm):
    i = pl.program_id(0)
    nw = nw_ref[...]
    hh = _rmsnorm(halo_ref[...], nw)
    hh = jnp.where(i % tiles_per_seq == 0, 0.0, hh)
    h_scr[0:HALO, :] = hh.astype(BF16)
    h_scr[HALO:, :] = _rmsnorm(res_ref[...], nw).astype(BF16)
    hx = h_scr[...]
    th = hid // FFN_SPLIT

    def branch(c0, slot):
        u_scr[slot] = jnp.dot(hx, wup_ref[:, c0:c0 + th], preferred_element_type=F32)
        cw = cw_ref[:, c0:c0 + th]
        out = cb_ref[:, c0:c0 + th]
        for kk in range(FFN_CONV):
            off = HALO - (FFN_CONV - 1) + kk
            out = out + cw[kk:kk + 1] * u_scr[slot, off:off + tm, :]
        return out

    total = res_ref[...]
    for c in range(FFN_SPLIT):
        act = (_silu(branch(hid + c * th, 2 * c)) * branch(c * th, 2 * c + 1)).astype(BF16)
        total = total + jnp.dot(act, wd_ref[c * th:(c + 1) * th, :], preferred_element_type=F32)
    o_ref[...] = _rmsnorm(total, fnw_ref[...]) if final_norm else total


def _ffn(res, nw, w_up, cw, cb, w_down, fnw, seq, tm, final_norm):
    t, d = res.shape
    hid = w_down.shape[0]
    row = lambda i: (i, 0)
    full = lambda i: (0, 0)
    once = pl.Buffered(1)
    halo_blocks = tm // HALO
    return pl.pallas_call(
        functools.partial(_ffn_kernel, tm=tm, hid=hid, tiles_per_seq=seq // tm,
                          final_norm=final_norm),
        grid=(t // tm,),
        in_specs=[pl.BlockSpec((tm, d), row),
                  pl.BlockSpec((HALO, d), lambda i: (jnp.maximum(i * halo_blocks - 1, 0), 0)),
                  pl.BlockSpec((1, d), full),
                  pl.BlockSpec(w_up.shape, full, pipeline_mode=once),
                  pl.BlockSpec(cw.shape, full, pipeline_mode=once),
                  pl.BlockSpec(cb.shape, full, pipeline_mode=once),
                  pl.BlockSpec(w_down.shape, full, pipeline_mode=once),
                  pl.BlockSpec((1, d), full)],
        out_specs=pl.BlockSpec((tm, d), row),
        out_shape=jax.ShapeDtypeStruct((t, d), F32),
        scratch_shapes=[pltpu.VMEM((HALO + tm, d), BF16),
                        pltpu.VMEM((2 * FFN_SPLIT, HALO + tm, hid // FFN_SPLIT), F32)],
        compiler_params=_cparams("parallel"),
        name="ffn",
    )(res, res, nw, w_up, cw, cb, w_down, fnw)


def kernel(x, mix_norm_w, w_in, gate_b, lambda_q1, lambda_k1, lambda_q2, lambda_k2, attn_subln_w,
           ssd_conv_w, ssd_conv_b, ssd_dt_bias, ssd_a_log, ssd_d, ssd_norm_w, w_attn_branch,
           w_ssd_branch, w_out, ffn_norm_w, w_up, ffn_conv_w, ffn_conv_b, w_down, final_norm_w):
    bsz, seq, d = x.shape
    t = bsz * seq
    depth = w_in.shape[0]
    att_w = ATT_HEADS * ATT_V_DIM
    inner = SSD_HEADS * SSD_HEAD_DIM
    cdim = inner + 2 * SSD_GROUPS * SSD_STATE
    c_q, c_k, c_v, c_z, c_x, c_dt = (att_w, 2 * att_w, 3 * att_w, 3 * att_w + inner,
                                     3 * att_w + inner + cdim, 3 * att_w + inner + cdim + SSD_HEADS)
    tk = min(ATT_KEYS, seq // 2)
    lc = min(SSD_TILE, seq)
    tm = min(ROW_TILE, seq)
    mm_rows = min(MM_ROWS, t)
    slopes = (jnp.exp2(-8.0 * jnp.arange(1, ATT_HEADS + 1, dtype=F32) / ATT_HEADS) * LOG2E).astype(F32)
    pad_heads = lambda v: jnp.pad(v.astype(F32), (0, LANES - SSD_HEADS)).reshape(1, LANES)

    res = x.reshape(t, d)
    for l in range(depth):
        lam_init = 0.8 - 0.6 * math.exp(-0.3 * l)
        wt = jnp.swapaxes(w_in[l], 0, 1).astype(BF16)
        wt_gate = wt[c_dt:]
        wdt = jnp.pad(wt[c_x:c_dt], ((0, LANES - SSD_HEADS), (0, 0)))

        h, dt_raw = _norm_dt(res, mix_norm_w[l].reshape(1, d), wdt, tm)
        blocks = lambda c0, c1: list(range(c0 // att_w, c1 // att_w))
        k = _matmul_nt(h, wt, blocks(c_q, c_k), att_w, mm_rows, "proj_k")
        z = _matmul_nt(h, wt, blocks(c_v, c_z), att_w, mm_rows, "proj_z")
        xbc = _matmul_nt(h, wt, blocks(c_z, c_x), att_w, mm_rows, "proj_xbc")
        gate_pre = _matmul_nt(h, wt_gate, blocks(0, wt_gate.shape[0]), att_w, mm_rows,
                              "proj_gate")
        qvt = _proj_t(wt, 0, c_k // att_w, att_w, ATT_HEAD_DIM ** -0.5 * LOG2E, h, bsz, seq, tk,
                      min(MM_ROWS, seq))

        lamv = jnp.stack([lambda_q1[l], lambda_k1[l], lambda_q2[l], lambda_k2[l]]).astype(F32)
        y_att = _attention(slopes, lamv, attn_subln_w[l].reshape(1, ATT_V_DIM).astype(F32), qvt,
                           k.reshape(bsz, seq, att_w), bsz, seq, tk, lam_init)
        y_ssd = _ssd(z.reshape(bsz, seq, inner), xbc.reshape(bsz, seq, cdim),
                     dt_raw.reshape(bsz, seq, LANES), ssd_conv_w[l].astype(F32),
                     ssd_conv_b[l].reshape(1, cdim).astype(F32), pad_heads(ssd_dt_bias[l]),
                     pad_heads(ssd_a_log[l]),
                     jnp.repeat(ssd_d[l].astype(F32), SSD_HEAD_DIM).reshape(1, inner),
                     ssd_norm_w[l].reshape(1, inner).astype(F32), lc)
        res = _merge(res, y_att.reshape(t, att_w), y_ssd.reshape(t, inner), gate_pre,
                     gate_b[l].reshape(1, -1).astype(F32), w_attn_branch[l].astype(BF16),
                     w_ssd_branch[l].astype(BF16), w_out[l].astype(BF16), tm)
        res = _ffn(res, ffn_norm_w[l].reshape(1, d).astype(F32), w_up[l].astype(BF16),
                   ffn_conv_w[l].astype(F32), ffn_conv_b[l].reshape(1, -1).astype(F32),
                   w_down[l].astype(BF16), final_norm_w.reshape(1, d).astype(F32), seq, tm,
                   l == depth - 1)
    return res.reshape(bsz, seq, d)
```

```python
import functools
import math

import jax
import jax.numpy as jnp
from jax import lax
from jax.experimental import pallas as pl
from jax.experimental.pallas import tpu as pltpu

F32 = jnp.float32
BF16 = jnp.bfloat16

EPS = 1e-6
LOG2E = math.log2(math.e)
NEG_BIG = -1e30

ATT_HEADS = 8
ATT_HEAD_DIM = 64
ATT_V_DIM = 2 * ATT_HEAD_DIM
ATT_CHUNK = 64
SSD_HEADS = 32
SSD_HEAD_DIM = 64
SSD_GROUPS = 4
SSD_STATE = 128
SSD_CONV = 4
FFN_CONV = 3

LANES = 128
VMEM_LIMIT_BYTES = 56 * 1024 * 1024

ATT_KEYS = 256
ATT_HEADS_PER_STEP = 2
SSD_TILE = 128
ROW_TILE = 512
MM_ROWS = 1024
FFN_SPLIT = 2
HALO = 16


def _cparams(*sem):
    return pltpu.CompilerParams(dimension_semantics=sem, vmem_limit_bytes=VMEM_LIMIT_BYTES)


def _sigmoid(x):
    return 0.5 * (jnp.tanh(0.5 * x) + 1.0)


def _silu(x):
    h = 0.5 * x
    return h * (jnp.tanh(h) + 1.0)


def _rmsnorm(x, w):
    ms = jnp.mean(x * x, axis=-1, keepdims=True)
    return x * lax.rsqrt(ms + EPS) * w


def _split3(x):
    hi = x.astype(BF16)
    r1 = x - hi.astype(F32)
    mid = r1.astype(BF16)
    lo = (r1 - mid.astype(F32)).astype(BF16)
    return hi, mid, lo


_NT = (((1,), (1,)), ((), ()))


def _norm_dt_kernel(x_ref, w_ref, wdt_ref, h_ref, dt_ref):
    h = _rmsnorm(x_ref[...], w_ref[...]).astype(BF16)
    h_ref[...] = h
    dt_ref[...] = lax.dot_general(h, wdt_ref[...], _NT, preferred_element_type=F32)


def _norm_dt(x, w, wdt, tm):
    t, d = x.shape
    return pl.pallas_call(
        _norm_dt_kernel,
        grid=(t // tm,),
        in_specs=[pl.BlockSpec((tm, d), lambda i: (i, 0)),
                  pl.BlockSpec((1, d), lambda i: (0, 0)),
                  pl.BlockSpec((LANES, d), lambda i: (0, 0))],
        out_specs=[pl.BlockSpec((tm, d), lambda i: (i, 0)),
                   pl.BlockSpec((tm, LANES), lambda i: (i, 0))],
        out_shape=[jax.ShapeDtypeStruct((t, d), BF16),
                   jax.ShapeDtypeStruct((t, LANES), F32)],
        compiler_params=_cparams("parallel"),
        name="norm_dt",
    )(x, w, wdt)


def _mm_nt_kernel(a_ref, *refs):
    w_refs, o_ref = refs[:-1], refs[-1]
    a = a_ref[...]
    for j, w_ref in enumerate(w_refs):
        n = w_ref.shape[0]
        o_ref[:, j * n:(j + 1) * n] = lax.dot_general(
            a, w_ref[...], _NT, preferred_element_type=F32).astype(o_ref.dtype)


def _matmul_nt(a, wt, row_blocks, blk, tm, name):
    m, k = a.shape
    n = blk * len(row_blocks)
    return pl.pallas_call(
        _mm_nt_kernel,
        grid=(m // tm,),
        in_specs=[pl.BlockSpec((tm, k), lambda i: (i, 0))]
        + [pl.BlockSpec((blk, k), lambda i, r=r: (r, 0)) for r in row_blocks],
        out_specs=pl.BlockSpec((tm, n), lambda i: (i, 0)),
        out_shape=jax.ShapeDtypeStruct((m, n), BF16),
        compiler_params=_cparams("parallel"),
        name=name,
    )(a, *([wt] * len(row_blocks)))


def _proj_t_kernel(wq_ref, wv_ref, h_ref, o_ref, *, tq, q_scale):
    nq = wq_ref.shape[0]
    for s in range(o_ref.shape[1]):
        hs = h_ref[s * tq:(s + 1) * tq, :]
        qt = lax.dot_general(wq_ref[...], hs, _NT, preferred_element_type=F32)
        o_ref[0, s, 0:nq, :] = (qt * q_scale).astype(BF16)
        o_ref[0, s, nq:, :] = lax.dot_general(wv_ref[...], hs, _NT,
                                              preferred_element_type=F32).astype(BF16)


def _proj_t(wt, q_block, v_block, blk, q_scale, h, bsz, seq, tq, tm):
    d = wt.shape[1]
    nq = seq // tq
    per_step = tm // tq
    steps = seq // tm
    return pl.pallas_call(
        functools.partial(_proj_t_kernel, tq=tq, q_scale=q_scale),
        grid=(bsz, steps),
        in_specs=[pl.BlockSpec((blk, d), lambda b, i: (q_block, 0)),
                  pl.BlockSpec((blk, d), lambda b, i: (v_block, 0)),
                  pl.BlockSpec((tm, d), lambda b, i: (b * steps + i, 0))],
        out_specs=pl.BlockSpec((1, per_step, 2 * blk, tq), lambda b, i: (b, i, 0, 0)),
        out_shape=jax.ShapeDtypeStruct((bsz, nq, 2 * blk, tq), BF16),
        compiler_params=_cparams("parallel", "parallel"),
        name="proj_t",
    )(wt, wt, h)


def _attn_kernel(slopes_ref, lamv_ref, subw_ref, qt_ref, k_ref, vt_ref, o_ref, rhs_ref, kaug_ref,
                 s_ref, p_ref, acc_ref, stat_ref, *, tk, lam_init):
    nh = ATT_HEADS_PER_STEP
    tq = 2 * tk
    hp = pl.program_id(1)
    qi = pl.program_id(2)
    slope = [slopes_ref[hp * nh + hh] for hh in range(nh)]

    mx_i = 0
    head_0 = 4 * nh
    n_stat = 6
    al_i, m_i, l_i = 0, 2, 4

    row = lax.broadcasted_iota(jnp.int32, (ATT_V_DIM, tq), 0)
    ridx = lax.broadcasted_iota(jnp.int32, (8, tq), 0)
    rpos = lax.broadcasted_iota(jnp.int32, (8, tq), 1).astype(F32)
    for hh in range(nh):
        a = slope[hh]
        hsl = slice(hh * ATT_V_DIM, (hh + 1) * ATT_V_DIM)
        qt = jnp.concatenate([qt_ref[0, 0, hsl, :], qt_ref[0, 1, hsl, :]], axis=1)
        zero = jnp.zeros_like(qt)
        val = jnp.where(ridx < 3, a, -a * rpos)
        hi = val.astype(BF16).astype(F32)
        r1 = val - hi
        mid = r1.astype(BF16).astype(F32)
        lo = r1 - mid
        sel = ridx % 3
        piece = jnp.where(sel == 0, hi, jnp.where(sel == 1, mid, lo))
        piece = jnp.where(ridx < 6, piece, 0.0)
        aug = jnp.concatenate([piece, jnp.zeros((LANES - 8, tq), F32)], axis=0).astype(BF16)
        rhs_ref[2 * hh] = jnp.concatenate([jnp.where(row < ATT_HEAD_DIM, qt, zero), aug], axis=0)
        rhs_ref[2 * hh + 1] = jnp.concatenate([jnp.where(row >= ATT_HEAD_DIM, qt, zero), aug],
                                              axis=0)
        for mi in range(2):
            base = head_0 + hh * n_stat
            stat_ref[base + m_i + mi] = jnp.full((8, tq), NEG_BIG, F32)
            stat_ref[base + l_i + mi] = jnp.zeros((8, tq), F32)

    cpos = lax.broadcasted_iota(jnp.int32, (tk, LANES), 0).astype(F32)
    lane = lax.broadcasted_iota(jnp.int32, (tk, LANES), 1)
    kaug_ref[...] = jnp.where(lane < 3, cpos, jnp.where(lane < 6, 1.0, 0.0)).astype(BF16)
    acc_ref[...] = jnp.zeros_like(acc_ref)

    groups = [(hh, mi) for hh in range(nh) for mi in range(2)]

    def scores(kt, hh, mi, diag):
        hsl = slice(hh * ATT_V_DIM, (hh + 1) * ATT_V_DIM)
        for u in range(2):
            start = pl.multiple_of((2 * kt + u) * tk, tk)
            kf = jnp.concatenate([k_ref[0, pl.ds(start, tk), hsl], kaug_ref[...]], axis=1)
            s = jnp.dot(kf, rhs_ref[2 * hh + mi], preferred_element_type=F32)
            if diag:
                ci = lax.broadcasted_iota(jnp.int32, (tk, tk), 0)
                ri = lax.broadcasted_iota(jnp.int32, (tk, tk), 1)
                fix = jnp.where(ci > ri, (2.0 * slope[hh]) * (ri - ci).astype(F32), 0.0)
                dextra = jnp.where((ci // ATT_CHUNK) <= (ri // ATT_CHUNK), fix, NEG_BIG)
                other = jnp.full((tk, tk), 0.0 if u == 0 else NEG_BIG, F32)
                s = s + jnp.concatenate([dextra, other] if u == 0 else [other, dextra], axis=1)
            sb = 4 * hh + 2 * mi + u
            s_ref[sb] = s
            stat_ref[mx_i + sb, 0:1, :] = jnp.max(s, axis=0, keepdims=True)

    def softmax(kt, hh, mi):
        base = head_0 + hh * n_stat
        cj = [-slope[hh] * (qi * tq - (2 * kt + u) * tk).astype(F32) for u in range(2)]
        sb = 4 * hh + 2 * mi
        m_old = stat_ref[base + m_i + mi, 0:1, :]
        mx = jnp.maximum(stat_ref[mx_i + sb, 0:1, :] + cj[0],
                         stat_ref[mx_i + sb + 1, 0:1, :] + cj[1])
        m_new = jnp.maximum(m_old, mx)
        alpha = jnp.exp2(m_old - m_new)
        lsum = alpha * stat_ref[base + l_i + mi, 0:1, :]
        for u in range(2):
            p = jnp.exp2(s_ref[sb + u] - (m_new - cj[u]))
            lsum = lsum + jnp.sum(p, axis=0, keepdims=True)
            p_ref[2 * hh + mi, u * tk:(u + 1) * tk, :] = p.astype(BF16)
        stat_ref[base + l_i + mi, 0:1, :] = lsum
        stat_ref[base + m_i + mi, 0:1, :] = m_new
        stat_ref[base + al_i + mi, 0:1, :] = alpha

    def values(kt, hh, mi):
        hsl = slice(hh * ATT_V_DIM, (hh + 1) * ATT_V_DIM)
        vt = jnp.concatenate([vt_ref[0, 2 * kt, hsl, :], vt_ref[0, 2 * kt + 1, hsl, :]], axis=1)
        acc_ref[2 * hh + mi] = (
            stat_ref[head_0 + hh * n_stat + al_i + mi, 0:1, :] * acc_ref[2 * hh + mi]
            + jnp.dot(vt, p_ref[2 * hh + mi], preferred_element_type=F32))

    for hh, mi in groups:
        scores(qi, hh, mi, True)

    @pl.when(qi >= 1)
    def _():
        for hh, mi in groups:
            softmax(qi, hh, mi)
            scores(0, hh, mi, False)

    def time_step(s):
        kv = jnp.where(s == 2, qi, s - 3)
        for hh, mi in groups:
            values(kv, hh, mi)
            softmax(s - 2, hh, mi)
            scores(s - 1, hh, mi, False)

    def body2(i, carry):
        time_step(2 * i + 2)
        time_step(2 * i + 3)
        return carry

    def body1(s, carry):
        time_step(s)
        return carry

    pairs = jnp.maximum(qi - 1, 0) // 2
    lax.fori_loop(0, pairs, body2, 0)
    lax.fori_loop(2 * pairs + 2, qi + 1, body1, 0)

    @pl.when(qi >= 1)
    def _():
        older = jnp.where(qi == 1, qi, qi - 2)
        for hh, mi in groups:
            values(older, hh, mi)

    last = jnp.where(qi == 0, qi, qi - 1)
    for hh, mi in groups:
        softmax(last, hh, mi)
        values(last, hh, mi)

    lv = lamv_ref[...]
    t1 = jnp.sum(lv[0:1] * lv[1:2], axis=1, keepdims=True)
    t2 = jnp.sum(lv[2:3] * lv[3:4], axis=1, keepdims=True)
    lam = jnp.exp(t1) - jnp.exp(t2) + lam_init
    subw = subw_ref[...] * (1.0 - lam_init)
    for hh in range(nh):
        base = head_0 + hh * n_stat
        o = (acc_ref[2 * hh] * (1.0 / stat_ref[base + l_i, 0:1, :])
             - lam * (acc_ref[2 * hh + 1] * (1.0 / stat_ref[base + l_i + 1, 0:1, :])))
        o = o * lax.rsqrt(jnp.mean(o * o, axis=0, keepdims=True) + EPS)
        o_ref[0, :, hh * ATT_V_DIM:(hh + 1) * ATT_V_DIM] = (o.T * subw).astype(BF16)


def _attention(slopes, lamv, subw, qvt, k, bsz, seq, tk, lam_init):
    nh = ATT_HEADS_PER_STEP
    tq = 2 * tk
    nq = seq // tq
    nk = seq // tk
    d = ATT_HEADS * ATT_V_DIM
    hw = nh * ATT_V_DIM
    return pl.pallas_call(
        functools.partial(_attn_kernel, tk=tk, lam_init=lam_init),
        grid=(bsz, ATT_HEADS // nh, nq),
        in_specs=[pl.BlockSpec(memory_space=pltpu.SMEM),
                  pl.BlockSpec((4, ATT_HEAD_DIM), lambda b, h, i: (0, 0)),
                  pl.BlockSpec((1, ATT_V_DIM), lambda b, h, i: (0, 0)),
                  pl.BlockSpec((1, 2, hw, tk), lambda b, h, i: (b, i, h, 0)),
                  pl.BlockSpec((1, seq, hw), lambda b, h, i: (b, 0, h)),
                  pl.BlockSpec((1, nk, hw, tk), lambda b, h, i: (b, 0, ATT_HEADS // nh + h, 0))],
        out_specs=pl.BlockSpec((1, tq, hw), lambda b, h, i: (b, i, h)),
        out_shape=jax.ShapeDtypeStruct((bsz, seq, d), BF16),
        scratch_shapes=[pltpu.VMEM((2 * nh, 2 * LANES, tq), BF16),
                        pltpu.VMEM((tk, LANES), BF16),
                        pltpu.VMEM((4 * nh, tk, tq), F32),
                        pltpu.VMEM((2 * nh, tq, tq), BF16),
                        pltpu.VMEM((2 * nh, ATT_V_DIM, tq), F32),
                        pltpu.VMEM((10 * nh, 8, tq), F32)],
        compiler_params=_cparams("parallel", "parallel", "arbitrary"),
        name="attn",
    )(slopes, lamv, subw, qvt, k, qvt)


def _ssd_kernel(z_ref, xbc_ref, dt_ref, cw_ref, cb_ref, dtb_ref, alog_ref, dexp_ref, nw_ref,
                y_ref, h_scr, xpad_scr, *, lc):
    inner = SSD_HEADS * SSD_HEAD_DIM
    gw = inner // SSD_GROUPS
    bc0 = inner
    cc0 = inner + SSD_GROUPS * SSD_STATE

    @pl.when(pl.program_id(1) == 0)
    def _():
        h_scr[...] = jnp.zeros_like(h_scr)
        xpad_scr[0:HALO, :] = jnp.zeros((HALO, xpad_scr.shape[1]), BF16)

    xraw = xbc_ref[0]
    xpad_scr[HALO:, :] = xraw
    xpad = xpad_scr[...]
    cw = cw_ref[...]
    conv = cb_ref[...] + cw[SSD_CONV - 1:SSD_CONV] * xraw.astype(F32)
    srow = lax.broadcasted_iota(jnp.int32, (lc, HALO + lc), 0)
    scol = lax.broadcasted_iota(jnp.int32, (lc, HALO + lc), 1)
    for kk in range(SSD_CONV - 1):
        shift = jnp.where(scol == srow + (HALO - (SSD_CONV - 1) + kk), 1.0, 0.0).astype(BF16)
        conv = conv + cw[kk:kk + 1] * jnp.dot(shift, xpad, preferred_element_type=F32)
    xpad_scr[0:HALO, :] = xraw[lc - HALO:, :]
    xc = _silu(conv)
    xs = xc[:, :inner]

    x = dt_ref[0] + dtb_ref[...]
    dt = jnp.maximum(x, 0.0) + jnp.log(1.0 + jnp.exp(-jnp.abs(x)))
    a = dt * (-jnp.exp(alog_ref[...]))

    ii = lax.broadcasted_iota(jnp.int32, (lc, lc), 0)
    jj = lax.broadcasted_iota(jnp.int32, (lc, lc), 1)
    tril = jj <= ii
    tri = jnp.where(tril, 1.0, 0.0).astype(BF16)
    a3 = jnp.concatenate(_split3(a), axis=1)
    c3 = jnp.dot(tri, a3, preferred_element_type=F32)
    acum = c3[:, :LANES] + c3[:, LANES:2 * LANES] + c3[:, 2 * LANES:]

    er = lax.broadcasted_iota(jnp.int32, (LANES, inner), 0)
    ec = lax.broadcasted_iota(jnp.int32, (LANES, inner), 1)
    expand = jnp.where(ec // SSD_HEAD_DIM == er, 1.0, 0.0).astype(BF16)
    acum_hi, acum_mid, _ = _split3(acum)
    stack = jnp.concatenate([acum_hi, acum_mid, dt.astype(BF16)], axis=0)
    ex = jnp.dot(stack, expand, preferred_element_type=F32)
    acum_e = ex[:lc] + ex[lc:2 * lc]
    dt_e = ex[2 * lc:]

    acum_last = acum_e[lc - 1:lc, :]
    xd = xs * dt_e
    xd_b = xd.astype(BF16)
    xd_end = (xd * jnp.exp(acum_last - acum_e)).astype(BF16)
    dec_in = jnp.exp(acum_e)
    dec_all = jnp.exp(acum_last)
    acum_t = acum.T

    lane = lax.broadcasted_iota(jnp.int32, (lc, LANES), 1)
    y_parts = []
    for g in range(SSD_GROUPS):
        bg = xc[:, bc0 + g * SSD_STATE:bc0 + (g + 1) * SSD_STATE]
        cg = xc[:, cc0 + g * SSD_STATE:cc0 + (g + 1) * SSD_STATE].astype(BF16)
        gsl = slice(g * gw, (g + 1) * gw)
        cbt = lax.dot_general(cg, bg.astype(BF16), (((1,), (1,)), ((), ())),
                              preferred_element_type=F32)
        h_prev = h_scr[:, gsl]
        y_off = jnp.dot(cg, h_prev.astype(BF16), preferred_element_type=F32) * dec_in[:, gsl]
        st = jnp.dot(bg.T.astype(BF16), xd_end[:, gsl], preferred_element_type=F32)
        h_scr[:, gsl] = h_prev * dec_all[:, gsl] + st
        heads_per_group = SSD_HEADS // SSD_GROUPS
        for pr in range(heads_per_group // 2):
            r0 = g * heads_per_group + 2 * pr
            ms = []
            for r in (r0, r0 + 1):
                seg = acum[:, r:r + 1] - acum_t[r:r + 1, :]
                ms.append((cbt * jnp.exp(jnp.where(tril, seg, NEG_BIG))).astype(BF16))
            xp = xd_b[:, r0 * SSD_HEAD_DIM:r0 * SSD_HEAD_DIM + LANES]
            zero = jnp.zeros_like(xp)
            xblk = jnp.concatenate([jnp.where(lane < SSD_HEAD_DIM, xp, zero),
                                    jnp.where(lane >= SSD_HEAD_DIM, xp, zero)], axis=0)
            y_diag = jnp.dot(jnp.concatenate(ms, axis=1), xblk, preferred_element_type=F32)
            lsl = slice(pr * LANES, (pr + 1) * LANES)
            y_parts.append(y_diag + y_off[:, lsl])
    y = jnp.concatenate(y_parts, axis=1) + xs * dexp_ref[...]

    gated = y * _silu(z_ref[0].astype(F32))
    outs = []
    for g in range(SSD_GROUPS):
        gg = gated[:, g * gw:(g + 1) * gw]
        outs.append(gg * lax.rsqrt(jnp.mean(gg * gg, axis=-1, keepdims=True) + EPS))
    y_ref[0] = (jnp.concatenate(outs, axis=1) * nw_ref[...]).astype(BF16)


def _ssd(z, xbc, dt, cw, cb, dtb, alog, dexp, nw, lc):
    bsz, seq, inner = z.shape
    cdim = xbc.shape[-1]
    full = lambda b, c: (0, 0)
    return pl.pallas_call(
        functools.partial(_ssd_kernel, lc=lc),
        grid=(bsz, seq // lc),
        in_specs=[pl.BlockSpec((1, lc, inner), lambda b, c: (b, c, 0)),
                  pl.BlockSpec((1, lc, cdim), lambda b, c: (b, c, 0)),
                  pl.BlockSpec((1, lc, LANES), lambda b, c: (b, c, 0)),
                  pl.BlockSpec((SSD_CONV, cdim), full),
                  pl.BlockSpec((1, cdim), full),
                  pl.BlockSpec((1, LANES), full),
                  pl.BlockSpec((1, LANES), full),
                  pl.BlockSpec((1, inner), full),
                  pl.BlockSpec((1, inner), full)],
        out_specs=pl.BlockSpec((1, lc, inner), lambda b, c: (b, c, 0)),
        out_shape=jax.ShapeDtypeStruct((bsz, seq, inner), BF16),
        scratch_shapes=[pltpu.VMEM((SSD_STATE, inner), F32), pltpu.VMEM((HALO + lc, cdim), BF16)],
        compiler_params=_cparams("parallel", "arbitrary"),
        name="ssd",
    )(z, xbc, dt, cw, cb, dtb, alog, dexp, nw)


def _merge_kernel(x_ref, ya_ref, ys_ref, gp_ref, gb_ref, wa_ref, ws_ref, wo_ref, o_ref):
    d = x_ref.shape[-1]
    gates = _sigmoid(gp_ref[...].astype(F32) + gb_ref[...])
    pa = jnp.dot(ya_ref[...], wa_ref[...], preferred_element_type=F32)
    ps = jnp.dot(ys_ref[...], ws_ref[...], preferred_element_type=F32)
    merged = gates[:, :d] * pa + gates[:, d:] * ps
    o_ref[...] = x_ref[...] + jnp.dot(merged.astype(BF16), wo_ref[...], preferred_element_type=F32)


def _merge(x, ya, ys, gp, gb, wa, ws, wo, tm):
    t, d = x.shape
    row = lambda i: (i, 0)
    full = lambda i: (0, 0)
    return pl.pallas_call(
        _merge_kernel,
        grid=(t // tm,),
        in_specs=[pl.BlockSpec((tm, d), row),
                  pl.BlockSpec((tm, ya.shape[1]), row),
                  pl.BlockSpec((tm, ys.shape[1]), row),
                  pl.BlockSpec((tm, gp.shape[1]), row),
                  pl.BlockSpec(gb.shape, full),
                  pl.BlockSpec(wa.shape, full),
                  pl.BlockSpec(ws.shape, full),
                  pl.BlockSpec(wo.shape, full)],
        out_specs=pl.BlockSpec((tm, d), row),
        out_shape=jax.ShapeDtypeStruct((t, d), F32),
        compiler_params=_cparams("parallel"),
        name="merge",
    )(x, ya, ys, gp, gb, wa, ws, wo)


def _ffn_kernel(res_ref, halo_ref, nw_ref, wup_ref, cw_ref, cb_ref, wd_ref, fnw_ref, o_ref, h_scr,
                u_scr, *, tm, hid, tiles_per_seq, final_norm):
    i = pl.program_id(0)
    nw = nw_ref[...]
    hh = _rmsnorm(halo_ref[...], nw)
    hh = jnp.where(i % tiles_per_seq == 0, 0.0, hh)
    h_scr[0:HALO, :] = hh.astype(BF16)
    h_scr[HALO:, :] = _rmsnorm(res_ref[...], nw).astype(BF16)
    hx = h_scr[...]
    th = hid // FFN_SPLIT

    def branch(c0, slot):
        u_scr[slot] = jnp.dot(hx, wup_ref[:, c0:c0 + th], preferred_element_type=F32)
        cw = cw_ref[:, c0:c0 + th]
        out = cb_ref[:, c0:c0 + th]
        for kk in range(FFN_CONV):
            off = HALO - (FFN_CONV - 1) + kk
            out = out + cw[kk:kk + 1] * u_scr[slot, off:off + tm, :]
        return out

    total = res_ref[...]
    for c in range(FFN_SPLIT):
        act = (_silu(branch(hid + c * th, 2 * c)) * branch(c * th, 2 * c + 1)).astype(BF16)
        total = total + jnp.dot(act, wd_ref[c * th:(c + 1) * th, :], preferred_element_type=F32)
    o_ref[...] = _rmsnorm(total, fnw_ref[...]) if final_norm else total


def _ffn(res, nw, w_up, cw, cb, w_down, fnw, seq, tm, final_norm):
    t, d = res.shape
    hid = w_down.shape[0]
    row = lambda i: (i, 0)
    full = lambda i: (0, 0)
    once = pl.Buffered(1)
    halo_blocks = tm // HALO
    return pl.pallas_call(
        functools.partial(_ffn_kernel, tm=tm, hid=hid, tiles_per_seq=seq // tm,
                          final_norm=final_norm),
        grid=(t // tm,),
        in_specs=[pl.BlockSpec((tm, d), row),
                  pl.BlockSpec((HALO, d), lambda i: (jnp.maximum(i * halo_blocks - 1, 0), 0)),
                  pl.BlockSpec((1, d), full),
                  pl.BlockSpec(w_up.shape, full, pipeline_mode=once),
                  pl.BlockSpec(cw.shape, full, pipeline_mode=once),
                  pl.BlockSpec(cb.shape, full, pipeline_mode=once),
                  pl.BlockSpec(w_down.shape, full, pipeline_mode=once),
                  pl.BlockSpec((1, d), full)],
        out_specs=pl.BlockSpec((tm, d), row),
        out_shape=jax.ShapeDtypeStruct((t, d), F32),
        scratch_shapes=[pltpu.VMEM((HALO + tm, d), BF16),
                        pltpu.VMEM((2 * FFN_SPLIT, HALO + tm, hid // FFN_SPLIT), F32)],
        compiler_params=_cparams("parallel"),
        name="ffn",
    )(res, res, nw, w_up, cw, cb, w_down, fnw)


def kernel(x, mix_norm_w, w_in, gate_b, lambda_q1, lambda_k1, lambda_q2, lambda_k2, attn_subln_w,
           ssd_conv_w, ssd_conv_b, ssd_dt_bias, ssd_a_log, ssd_d, ssd_norm_w, w_attn_branch,
           w_ssd_branch, w_out, ffn_norm_w, w_up, ffn_conv_w, ffn_conv_b, w_down, final_norm_w):
    bsz, seq, d = x.shape
    t = bsz * seq
    depth = w_in.shape[0]
    att_w = ATT_HEADS * ATT_V_DIM
    inner = SSD_HEADS * SSD_HEAD_DIM
    cdim = inner + 2 * SSD_GROUPS * SSD_STATE
    c_q, c_k, c_v, c_z, c_x, c_dt = (att_w, 2 * att_w, 3 * att_w, 3 * att_w + inner,
                                     3 * att_w + inner + cdim, 3 * att_w + inner + cdim + SSD_HEADS)
    tk = min(ATT_KEYS, seq // 2)
    lc = min(SSD_TILE, seq)
    tm = min(ROW_TILE, seq)
    mm_rows = min(MM_ROWS, t)
    slopes = (jnp.exp2(-8.0 * jnp.arange(1, ATT_HEADS + 1, dtype=F32) / ATT_HEADS) * LOG2E).astype(F32)
    pad_heads = lambda v: jnp.pad(v.astype(F32), (0, LANES - SSD_HEADS)).reshape(1, LANES)

    res = x.reshape(t, d)
    for l in range(depth):
        lam_init = 0.8 - 0.6 * math.exp(-0.3 * l)
        wt = jnp.swapaxes(w_in[l], 0, 1).astype(BF16)
        wt_gate = wt[c_dt:]
        wdt = jnp.pad(wt[c_x:c_dt], ((0, LANES - SSD_HEADS), (0, 0)))

        h, dt_raw = _norm_dt(res, mix_norm_w[l].reshape(1, d), wdt, tm)
        blocks = lambda c0, c1: list(range(c0 // att_w, c1 // att_w))
        k = _matmul_nt(h, wt, blocks(c_q, c_k), att_w, mm_rows, "proj_k")
        z = _matmul_nt(h, wt, blocks(c_v, c_z), att_w, mm_rows, "proj_z")
        xbc = _matmul_nt(h, wt, blocks(c_z, c_x), att_w, mm_rows, "proj_xbc")
        gate_pre = _matmul_nt(h, wt_gate, blocks(0, wt_gate.shape[0]), att_w, mm_rows,
                              "proj_gate")
        qvt = _proj_t(wt, 0, c_k // att_w, att_w, ATT_HEAD_DIM ** -0.5 * LOG2E, h, bsz, seq, tk,
                      min(MM_ROWS, seq))

        lamv = jnp.stack([lambda_q1[l], lambda_k1[l], lambda_q2[l], lambda_k2[l]]).astype(F32)
        y_att = _attention(slopes, lamv, attn_subln_w[l].reshape(1, ATT_V_DIM).astype(F32), qvt,
                           k.reshape(bsz, seq, att_w), bsz, seq, tk, lam_init)
        y_ssd = _ssd(z.reshape(bsz, seq, inner), xbc.reshape(bsz, seq, cdim),
                     dt_raw.reshape(bsz, seq, LANES), ssd_conv_w[l].astype(F32),
                     ssd_conv_b[l].reshape(1, cdim).astype(F32), pad_heads(ssd_dt_bias[l]),
                     pad_heads(ssd_a_log[l]),
                     jnp.repeat(ssd_d[l].astype(F32), SSD_HEAD_DIM).reshape(1, inner),
                     ssd_norm_w[l].reshape(1, inner).astype(F32), lc)
        res = _merge(res, y_att.reshape(t, att_w), y_ssd.reshape(t, inner), gate_pre,
                     gate_b[l].reshape(1, -1).astype(F32), w_attn_branch[l].astype(BF16),
                     w_ssd_branch[l].astype(BF16), w_out[l].astype(BF16), tm)
        res = _ffn(res, ffn_norm_w[l].reshape(1, d).astype(F32), w_up[l].astype(BF16),
                   ffn_conv_w[l].astype(F32), ffn_conv_b[l].reshape(1, -1).astype(F32),
                   w_down[l].astype(BF16), final_norm_w.reshape(1, d).astype(F32), seq, tm,
                   l == depth - 1)
    return res.reshape(bsz, seq, d)
```

```python
import functools
import math

import jax
import jax.numpy as jnp
from jax import lax
from jax.experimental import pallas as pl
from jax.experimental.pallas import tpu as pltpu

F32 = jnp.float32
BF16 = jnp.bfloat16

EPS = 1e-6
LOG2E = math.log2(math.e)
NEG_BIG = -1e30

ATT_HEADS = 8
ATT_HEAD_DIM = 64
ATT_V_DIM = 2 * ATT_HEAD_DIM
ATT_CHUNK = 64
SSD_HEADS = 32
SSD_HEAD_DIM = 64
SSD_GROUPS = 4
SSD_STATE = 128
SSD_CONV = 4
FFN_CONV = 3

LANES = 128
VMEM_LIMIT_BYTES = 56 * 1024 * 1024

ATT_KEYS = 256
ATT_HEADS_PER_STEP = 2
SSD_TILE = 128
ROW_TILE = 512
MM_ROWS = 1024
FFN_SPLIT = 1
FFN_ROW_PARTS = 1
HALO = 16


def _cparams(*sem):
    return pltpu.CompilerParams(dimension_semantics=sem, vmem_limit_bytes=VMEM_LIMIT_BYTES)


def _sigmoid(x):
    return 0.5 * (jnp.tanh(0.5 * x) + 1.0)


def _silu(x):
    h = 0.5 * x
    return h * (jnp.tanh(h) + 1.0)


def _rmsnorm(x, w):
    ms = jnp.mean(x * x, axis=-1, keepdims=True)
    return x * lax.rsqrt(ms + EPS) * w


def _split3(x):
    hi = x.astype(BF16)
    r1 = x - hi.astype(F32)
    mid = r1.astype(BF16)
    lo = (r1 - mid.astype(F32)).astype(BF16)
    return hi, mid, lo


_NT = (((1,), (1,)), ((), ()))


def _norm_dt_kernel(x_ref, w_ref, wdt_ref, h_ref, dt_ref):
    h = _rmsnorm(x_ref[...], w_ref[...]).astype(BF16)
    h_ref[...] = h
    dt_ref[...] = lax.dot_general(h, wdt_ref[...], _NT, preferred_element_type=F32)


def _norm_dt(x, w, wdt, tm):
    t, d = x.shape
    return pl.pallas_call(
        _norm_dt_kernel,
        grid=(t // tm,),
        in_specs=[pl.BlockSpec((tm, d), lambda i: (i, 0)),
                  pl.BlockSpec((1, d), lambda i: (0, 0)),
                  pl.BlockSpec((LANES, d), lambda i: (0, 0))],
        out_specs=[pl.BlockSpec((tm, d), lambda i: (i, 0)),
                   pl.BlockSpec((tm, LANES), lambda i: (i, 0))],
        out_shape=[jax.ShapeDtypeStruct((t, d), BF16),
                   jax.ShapeDtypeStruct((t, LANES), F32)],
        compiler_params=_cparams("parallel"),
        name="norm_dt",
    )(x, w, wdt)


def _mm_nt_kernel(a_ref, *refs):
    w_refs, o_ref = refs[:-1], refs[-1]
    a = a_ref[...]
    for j, w_ref in enumerate(w_refs):
        n = w_ref.shape[0]
        o_ref[:, j * n:(j + 1) * n] = lax.dot_general(
            a, w_ref[...], _NT, preferred_element_type=F32).astype(o_ref.dtype)


def _matmul_nt(a, wt, row_blocks, blk, tm, name):
    m, k = a.shape
    n = blk * len(row_blocks)
    return pl.pallas_call(
        _mm_nt_kernel,
        grid=(m // tm,),
        in_specs=[pl.BlockSpec((tm, k), lambda i: (i, 0))]
        + [pl.BlockSpec((blk, k), lambda i, r=r: (r, 0)) for r in row_blocks],
        out_specs=pl.BlockSpec((tm, n), lambda i: (i, 0)),
        out_shape=jax.ShapeDtypeStruct((m, n), BF16),
        compiler_params=_cparams("parallel"),
        name=name,
    )(a, *([wt] * len(row_blocks)))


def _proj_t_kernel(wq_ref, wv_ref, h_ref, o_ref, *, tq, q_scale):
    nq = wq_ref.shape[0]
    for s in range(o_ref.shape[1]):
        hs = h_ref[s * tq:(s + 1) * tq, :]
        qt = lax.dot_general(wq_ref[...], hs, _NT, preferred_element_type=F32)
        o_ref[0, s, 0:nq, :] = (qt * q_scale).astype(BF16)
        o_ref[0, s, nq:, :] = lax.dot_general(wv_ref[...], hs, _NT,
                                              preferred_element_type=F32).astype(BF16)


def _proj_t(wt, q_block, v_block, blk, q_scale, h, bsz, seq, tq, tm):
    d = wt.shape[1]
    nq = seq // tq
    per_step = tm // tq
    steps = seq // tm
    return pl.pallas_call(
        functools.partial(_proj_t_kernel, tq=tq, q_scale=q_scale),
        grid=(bsz, steps),
        in_specs=[pl.BlockSpec((blk, d), lambda b, i: (q_block, 0)),
                  pl.BlockSpec((blk, d), lambda b, i: (v_block, 0)),
                  pl.BlockSpec((tm, d), lambda b, i: (b * steps + i, 0))],
        out_specs=pl.BlockSpec((1, per_step, 2 * blk, tq), lambda b, i: (b, i, 0, 0)),
        out_shape=jax.ShapeDtypeStruct((bsz, nq, 2 * blk, tq), BF16),
        compiler_params=_cparams("parallel", "parallel"),
        name="proj_t",
    )(wt, wt, h)


def _attn_kernel(slopes_ref, lamv_ref, subw_ref, qt_ref, k_ref, vt_ref, o_ref, rhs_ref, kaug_ref,
                 s_ref, p_ref, acc_ref, stat_ref, *, tk, lam_init):
    nh = ATT_HEADS_PER_STEP
    tq = 2 * tk
    hp = pl.program_id(1)
    qi = pl.program_id(2)
    slope = [slopes_ref[hp * nh + hh] for hh in range(nh)]

    mx_i = 0
    head_0 = 4 * nh
    n_stat = 6
    al_i, m_i, l_i = 0, 2, 4

    row = lax.broadcasted_iota(jnp.int32, (ATT_V_DIM, tq), 0)
    ridx = lax.broadcasted_iota(jnp.int32, (8, tq), 0)
    rpos = lax.broadcasted_iota(jnp.int32, (8, tq), 1).astype(F32)
    for hh in range(nh):
        a = slope[hh]
        hsl = slice(hh * ATT_V_DIM, (hh + 1) * ATT_V_DIM)
        qt = jnp.concatenate([qt_ref[0, 0, hsl, :], qt_ref[0, 1, hsl, :]], axis=1)
        zero = jnp.zeros_like(qt)
        val = jnp.where(ridx < 3, a, -a * rpos)
        hi = val.astype(BF16).astype(F32)
        r1 = val - hi
        mid = r1.astype(BF16).astype(F32)
        lo = r1 - mid
        sel = ridx % 3
        piece = jnp.where(sel == 0, hi, jnp.where(sel == 1, mid, lo))
        piece = jnp.where(ridx < 6, piece, 0.0)
        aug = jnp.concatenate([piece, jnp.zeros((LANES - 8, tq), F32)], axis=0).astype(BF16)
        rhs_ref[2 * hh] = jnp.concatenate([jnp.where(row < ATT_HEAD_DIM, qt, zero), aug], axis=0)
        rhs_ref[2 * hh + 1] = jnp.concatenate([jnp.where(row >= ATT_HEAD_DIM, qt, zero), aug],
                                              axis=0)
        for mi in range(2):
            base = head_0 + hh * n_stat
            stat_ref[base + m_i + mi] = jnp.full((8, tq), NEG_BIG, F32)
            stat_ref[base + l_i + mi] = jnp.zeros((8, tq), F32)

    cpos = lax.broadcasted_iota(jnp.int32, (tk, LANES), 0).astype(F32)
    lane = lax.broadcasted_iota(jnp.int32, (tk, LANES), 1)
    kaug_ref[...] = jnp.where(lane < 3, cpos, jnp.where(lane < 6, 1.0, 0.0)).astype(BF16)
    acc_ref[...] = jnp.zeros_like(acc_ref)

    groups = [(hh, mi) for hh in range(nh) for mi in range(2)]

    def scores(kt, hh, mi, diag):
        hsl = slice(hh * ATT_V_DIM, (hh + 1) * ATT_V_DIM)
        for u in range(2):
            start = pl.multiple_of((2 * kt + u) * tk, tk)
            kf = jnp.concatenate([k_ref[0, pl.ds(start, tk), hsl], kaug_ref[...]], axis=1)
            s = jnp.dot(kf, rhs_ref[2 * hh + mi], preferred_element_type=F32)
            if diag:
                ci = lax.broadcasted_iota(jnp.int32, (tk, tk), 0)
                ri = lax.broadcasted_iota(jnp.int32, (tk, tk), 1)
                fix = jnp.where(ci > ri, (2.0 * slope[hh]) * (ri - ci).astype(F32), 0.0)
                dextra = jnp.where((ci // ATT_CHUNK) <= (ri // ATT_CHUNK), fix, NEG_BIG)
                other = jnp.full((tk, tk), 0.0 if u == 0 else NEG_BIG, F32)
                s = s + jnp.concatenate([dextra, other] if u == 0 else [other, dextra], axis=1)
            sb = 4 * hh + 2 * mi + u
            s_ref[sb] = s
            stat_ref[mx_i + sb, 0:1, :] = jnp.max(s, axis=0, keepdims=True)

    def softmax(kt, hh, mi):
        base = head_0 + hh * n_stat
        cj = [-slope[hh] * (qi * tq - (2 * kt + u) * tk).astype(F32) for u in range(2)]
        sb = 4 * hh + 2 * mi
        m_old = stat_ref[base + m_i + mi, 0:1, :]
        mx = jnp.maximum(stat_ref[mx_i + sb, 0:1, :] + cj[0],
                         stat_ref[mx_i + sb + 1, 0:1, :] + cj[1])
        m_new = jnp.maximum(m_old, mx)
        alpha = jnp.exp2(m_old - m_new)
        lsum = alpha * stat_ref[base + l_i + mi, 0:1, :]
        for u in range(2):
            p = jnp.exp2(s_ref[sb + u] - (m_new - cj[u]))
            lsum = lsum + jnp.sum(p, axis=0, keepdims=True)
            p_ref[2 * hh + mi, u * tk:(u + 1) * tk, :] = p.astype(BF16)
        stat_ref[base + l_i + mi, 0:1, :] = lsum
        stat_ref[base + m_i + mi, 0:1, :] = m_new
        stat_ref[base + al_i + mi, 0:1, :] = alpha

    def values(kt, hh, mi):
        hsl = slice(hh * ATT_V_DIM, (hh + 1) * ATT_V_DIM)
        vt = jnp.concatenate([vt_ref[0, 2 * kt, hsl, :], vt_ref[0, 2 * kt + 1, hsl, :]], axis=1)
        acc_ref[2 * hh + mi] = (
            stat_ref[head_0 + hh * n_stat + al_i + mi, 0:1, :] * acc_ref[2 * hh + mi]
            + jnp.dot(vt, p_ref[2 * hh + mi], preferred_element_type=F32))

    for hh, mi in groups:
        scores(qi, hh, mi, True)

    @pl.when(qi >= 1)
    def _():
        for hh, mi in groups:
            softmax(qi, hh, mi)
        for hh, mi in groups:
            scores(0, hh, mi, False)

    def time_step(s):
        kv = jnp.where(s == 2, qi, s - 3)
        for hh, mi in groups:
            values(kv, hh, mi)
        for hh, mi in groups:
            softmax(s - 2, hh, mi)
        for hh, mi in groups:
            scores(s - 1, hh, mi, False)

    def body2(i, carry):
        time_step(2 * i + 2)
        time_step(2 * i + 3)
        return carry

    def body1(s, carry):
        time_step(s)
        return carry

    pairs = jnp.maximum(qi - 1, 0) // 2
    lax.fori_loop(0, pairs, body2, 0)
    lax.fori_loop(jnp.minimum(2 * pairs + 2, qi + 1), qi + 1, body1, 0)

    @pl.when(qi >= 1)
    def _():
        older = jnp.where(qi == 1, qi, qi - 2)
        for hh, mi in groups:
            values(older, hh, mi)

    last = jnp.where(qi == 0, qi, qi - 1)
    for hh, mi in groups:
        softmax(last, hh, mi)
    for hh, mi in groups:
        values(last, hh, mi)

    lv = lamv_ref[...]
    t1 = jnp.sum(lv[0:1] * lv[1:2], axis=1, keepdims=True)
    t2 = jnp.sum(lv[2:3] * lv[3:4], axis=1, keepdims=True)
    lam = jnp.exp(t1) - jnp.exp(t2) + lam_init
    subw = subw_ref[...] * (1.0 - lam_init)
    for hh in range(nh):
        base = head_0 + hh * n_stat
        o = (acc_ref[2 * hh] * (1.0 / stat_ref[base + l_i, 0:1, :])
             - lam * (acc_ref[2 * hh + 1] * (1.0 / stat_ref[base + l_i + 1, 0:1, :])))
        o = o * lax.rsqrt(jnp.mean(o * o, axis=0, keepdims=True) + EPS)
        o_ref[0, hh * ATT_V_DIM:(hh + 1) * ATT_V_DIM, :] = (o * subw).astype(BF16)


def _attention(slopes, lamv, subw, qvt, k, bsz, seq, tk, lam_init):
    nh = ATT_HEADS_PER_STEP
    tq = 2 * tk
    nq = seq // tq
    nk = seq // tk
    d = ATT_HEADS * ATT_V_DIM
    hw = nh * ATT_V_DIM
    return pl.pallas_call(
        functools.partial(_attn_kernel, tk=tk, lam_init=lam_init),
        grid=(bsz, ATT_HEADS // nh, nq),
        in_specs=[pl.BlockSpec(memory_space=pltpu.SMEM),
                  pl.BlockSpec((4, ATT_HEAD_DIM), lambda b, h, i: (0, 0)),
                  pl.BlockSpec((ATT_V_DIM, 1), lambda b, h, i: (0, 0)),
                  pl.BlockSpec((1, 2, hw, tk), lambda b, h, i: (b, i, h, 0)),
                  pl.BlockSpec((1, seq, hw), lambda b, h, i: (b, 0, h)),
                  pl.BlockSpec((1, nk, hw, tk), lambda b, h, i: (b, 0, ATT_HEADS // nh + h, 0))],
        out_specs=pl.BlockSpec((1, hw, tq), lambda b, h, i: (b, h, i)),
        out_shape=jax.ShapeDtypeStruct((bsz, d, seq), BF16),
        scratch_shapes=[pltpu.VMEM((2 * nh, 2 * LANES, tq), BF16),
                        pltpu.VMEM((tk, LANES), BF16),
                        pltpu.VMEM((4 * nh, tk, tq), F32),
                        pltpu.VMEM((2 * nh, tq, tq), BF16),
                        pltpu.VMEM((2 * nh, ATT_V_DIM, tq), F32),
                        pltpu.VMEM((10 * nh, 8, tq), F32)],
        compiler_params=_cparams("parallel", "parallel", "arbitrary"),
        name="attn",
    )(slopes, lamv, subw, qvt, k, qvt)


def _ssd_kernel(z_ref, xbc_ref, dt_ref, cw_ref, cb_ref, dtb_ref, alog_ref, dexp_ref, nw_ref,
                y_ref, h_scr, xpad_scr, *, lc):
    inner = SSD_HEADS * SSD_HEAD_DIM
    gw = inner // SSD_GROUPS
    bc0 = inner
    cc0 = inner + SSD_GROUPS * SSD_STATE

    @pl.when(pl.program_id(1) == 0)
    def _():
        h_scr[...] = jnp.zeros_like(h_scr)
        xpad_scr[0:HALO, :] = jnp.zeros((HALO, xpad_scr.shape[1]), BF16)

    xraw = xbc_ref[0]
    xpad_scr[HALO:, :] = xraw
    xpad = xpad_scr[...]
    cw = cw_ref[...]
    conv = cb_ref[...] + cw[SSD_CONV - 1:SSD_CONV] * xraw.astype(F32)
    srow = lax.broadcasted_iota(jnp.int32, (lc, HALO + lc), 0)
    scol = lax.broadcasted_iota(jnp.int32, (lc, HALO + lc), 1)
    for kk in range(SSD_CONV - 1):
        shift = jnp.where(scol == srow + (HALO - (SSD_CONV - 1) + kk), 1.0, 0.0).astype(BF16)
        conv = conv + cw[kk:kk + 1] * jnp.dot(shift, xpad, preferred_element_type=F32)
    xpad_scr[0:HALO, :] = xraw[lc - HALO:, :]
    xc = _silu(conv)
    xs = xc[:, :inner]

    x = dt_ref[0] + dtb_ref[...]
    dt = jnp.maximum(x, 0.0) + jnp.log(1.0 + jnp.exp(-jnp.abs(x)))
    a = dt * (-jnp.exp(alog_ref[...]))

    ii = lax.broadcasted_iota(jnp.int32, (lc, lc), 0)
    jj = lax.broadcasted_iota(jnp.int32, (lc, lc), 1)
    tril = jj <= ii
    tri = jnp.where(tril, 1.0, 0.0).astype(BF16)
    a3 = jnp.concatenate(_split3(a), axis=1)
    c3 = jnp.dot(tri, a3, preferred_element_type=F32)
    acum = c3[:, :LANES] + c3[:, LANES:2 * LANES] + c3[:, 2 * LANES:]

    er = lax.broadcasted_iota(jnp.int32, (LANES, inner), 0)
    ec = lax.broadcasted_iota(jnp.int32, (LANES, inner), 1)
    expand = jnp.where(ec // SSD_HEAD_DIM == er, 1.0, 0.0).astype(BF16)
    dt_hi = dt.astype(BF16)
    dt_lo = (dt - dt_hi.astype(F32)).astype(BF16)
    stack = jnp.concatenate(list(_split3(acum)) + [dt_hi, dt_lo], axis=0)
    ex = jnp.dot(stack, expand, preferred_element_type=F32)
    acum_e = ex[:lc] + ex[lc:2 * lc] + ex[2 * lc:3 * lc]
    dt_e = ex[3 * lc:4 * lc] + ex[4 * lc:]

    acum_last = acum_e[lc - 1:lc, :]
    xd = xs * dt_e
    xd_b = xd.astype(BF16)
    xd_end = (xd * jnp.exp(acum_last - acum_e)).astype(BF16)
    dec_in = jnp.exp(acum_e)
    dec_all = jnp.exp(acum_last)
    acum_t = acum.T

    lane = lax.broadcasted_iota(jnp.int32, (lc, LANES), 1)
    y_parts = []
    for g in range(SSD_GROUPS):
        bg = xc[:, bc0 + g * SSD_STATE:bc0 + (g + 1) * SSD_STATE]
        cg = xc[:, cc0 + g * SSD_STATE:cc0 + (g + 1) * SSD_STATE].astype(BF16)
        gsl = slice(g * gw, (g + 1) * gw)
        cbt = lax.dot_general(cg, bg.astype(BF16), (((1,), (1,)), ((), ())),
                              preferred_element_type=F32)
        h_prev = h_scr[:, gsl]
        y_off = jnp.dot(cg, h_prev.astype(BF16), preferred_element_type=F32) * dec_in[:, gsl]
        st = jnp.dot(bg.T.astype(BF16), xd_end[:, gsl], preferred_element_type=F32)
        h_scr[:, gsl] = h_prev * dec_all[:, gsl] + st
        heads_per_group = SSD_HEADS // SSD_GROUPS
        for pr in range(heads_per_group // 2):
            r0 = g * heads_per_group + 2 * pr
            ms = []
            for r in (r0, r0 + 1):
                seg = acum[:, r:r + 1] - acum_t[r:r + 1, :]
                ms.append((cbt * jnp.exp(jnp.where(tril, seg, NEG_BIG))).astype(BF16))
            xp = xd_b[:, r0 * SSD_HEAD_DIM:r0 * SSD_HEAD_DIM + LANES]
            zero = jnp.zeros_like(xp)
            xblk = jnp.concatenate([jnp.where(lane < SSD_HEAD_DIM, xp, zero),
                                    jnp.where(lane >= SSD_HEAD_DIM, xp, zero)], axis=0)
            y_diag = jnp.dot(jnp.concatenate(ms, axis=1), xblk, preferred_element_type=F32)
            lsl = slice(pr * LANES, (pr + 1) * LANES)
            y_parts.append(y_diag + y_off[:, lsl])
    y = jnp.concatenate(y_parts, axis=1) + xs * dexp_ref[...]

    gated = y * _silu(z_ref[0].astype(F32))
    outs = []
    for g in range(SSD_GROUPS):
        gg = gated[:, g * gw:(g + 1) * gw]
        outs.append(gg * lax.rsqrt(jnp.mean(gg * gg, axis=-1, keepdims=True) + EPS))
    y_ref[0] = (jnp.concatenate(outs, axis=1) * nw_ref[...]).astype(BF16)


def _ssd(z, xbc, dt, cw, cb, dtb, alog, dexp, nw, lc):
    bsz, seq, inner = z.shape
    cdim = xbc.shape[-1]
    full = lambda b, c: (0, 0)
    return pl.pallas_call(
        functools.partial(_ssd_kernel, lc=lc),
        grid=(bsz, seq // lc),
        in_specs=[pl.BlockSpec((1, lc, inner), lambda b, c: (b, c, 0)),
                  pl.BlockSpec((1, lc, cdim), lambda b, c: (b, c, 0)),
                  pl.BlockSpec((1, lc, LANES), lambda b, c: (b, c, 0)),
                  pl.BlockSpec((SSD_CONV, cdim), full),
                  pl.BlockSpec((1, cdim), full),
                  pl.BlockSpec((1, LANES), full),
                  pl.BlockSpec((1, LANES), full),
                  pl.BlockSpec((1, inner), full),
                  pl.BlockSpec((1, inner), full)],
        out_specs=pl.BlockSpec((1, lc, inner), lambda b, c: (b, c, 0)),
        out_shape=jax.ShapeDtypeStruct((bsz, seq, inner), BF16),
        scratch_shapes=[pltpu.VMEM((SSD_STATE, inner), F32), pltpu.VMEM((HALO + lc, cdim), BF16)],
        compiler_params=_cparams("parallel", "arbitrary"),
        name="ssd",
    )(z, xbc, dt, cw, cb, dtb, alog, dexp, nw)


def _merge_kernel(x_ref, ya_ref, ys_ref, gp_ref, gb_ref, wa_ref, ws_ref, wo_ref, o_ref):
    d = x_ref.shape[-1]
    gates = _sigmoid(gp_ref[...].astype(F32) + gb_ref[...])
    pa = lax.dot_general(ya_ref[0], wa_ref[...], (((0,), (0,)), ((), ())),
                         preferred_element_type=F32)
    ps = jnp.dot(ys_ref[...], ws_ref[...], preferred_element_type=F32)
    merged = gates[:, :d] * pa + gates[:, d:] * ps
    o_ref[...] = x_ref[...] + jnp.dot(merged.astype(BF16), wo_ref[...], preferred_element_type=F32)


def _merge(x, ya, ys, gp, gb, wa, ws, wo, tm):
    t, d = x.shape
    tiles_per_seq = ya.shape[2] // tm
    row = lambda i: (i, 0)
    full = lambda i: (0, 0)
    return pl.pallas_call(
        _merge_kernel,
        grid=(t // tm,),
        in_specs=[pl.BlockSpec((tm, d), row),
                  pl.BlockSpec((1, ya.shape[1], tm),
                               lambda i: (i // tiles_per_seq, 0, i % tiles_per_seq)),
                  pl.BlockSpec((tm, ys.shape[1]), row),
                  pl.BlockSpec((tm, gp.shape[1]), row),
                  pl.BlockSpec(gb.shape, full),
                  pl.BlockSpec(wa.shape, full),
                  pl.BlockSpec(ws.shape, full),
                  pl.BlockSpec(wo.shape, full)],
        out_specs=pl.BlockSpec((tm, d), row),
        out_shape=jax.ShapeDtypeStruct((t, d), F32),
        compiler_params=_cparams("parallel"),
        name="merge",
    )(x, ya, ys, gp, gb, wa, ws, wo)


def _ffn_kernel(res_ref, halo_ref, nw_ref, wup_ref, cw_ref, cb_ref, wd_ref, fnw_ref, o_ref, h_scr,
                u_scr, *, tm, hid, tiles_per_seq, final_norm):
    i = pl.program_id(0)
    nw = nw_ref[...]
    hh = _rmsnorm(halo_ref[...], nw)
    hh = jnp.where(i % tiles_per_seq == 0, 0.0, hh)
    h_scr[0:HALO, :] = hh.astype(BF16)
    h_scr[HALO:, :] = _rmsnorm(res_ref[...], nw).astype(BF16)
    th = hid // FFN_SPLIT
    rows = tm // FFN_ROW_PARTS

    for part in range(FFN_ROW_PARTS):
        r0 = part * rows
        hx = h_scr[r0:r0 + HALO + rows, :]

        def branch(c0, slot):
            u_scr[slot] = jnp.dot(hx, wup_ref[:, c0:c0 + th], preferred_element_type=F32)
            cw = cw_ref[:, c0:c0 + th]
            out = cb_ref[:, c0:c0 + th]
            for kk in range(FFN_CONV):
                off = HALO - (FFN_CONV - 1) + kk
                out = out + cw[kk:kk + 1] * u_scr[slot, off:off + rows, :]
            return out

        total = res_ref[r0:r0 + rows, :]
        for c in range(FFN_SPLIT):
            slot = 2 * (part * FFN_SPLIT + c)
            act = (_silu(branch(hid + c * th, slot)) * branch(c * th, slot + 1)).astype(BF16)
            total = total + jnp.dot(act, wd_ref[c * th:(c + 1) * th, :],
                                    preferred_element_type=F32)
        o_ref[r0:r0 + rows, :] = _rmsnorm(total, fnw_ref[...]) if final_norm else total


def _ffn(res, nw, w_up, cw, cb, w_down, fnw, seq, tm, final_norm):
    t, d = res.shape
    hid = w_down.shape[0]
    row = lambda i: (i, 0)
    full = lambda i: (0, 0)
    once = pl.Buffered(1)
    halo_blocks = tm // HALO
    return pl.pallas_call(
        functools.partial(_ffn_kernel, tm=tm, hid=hid, tiles_per_seq=seq // tm,
                          final_norm=final_norm),
        grid=(t // tm,),
        in_specs=[pl.BlockSpec((tm, d), row),
                  pl.BlockSpec((HALO, d), lambda i: (jnp.maximum(i * halo_blocks - 1, 0), 0)),
                  pl.BlockSpec((1, d), full),
                  pl.BlockSpec(w_up.shape, full, pipeline_mode=once),
                  pl.BlockSpec(cw.shape, full, pipeline_mode=once),
                  pl.BlockSpec(cb.shape, full, pipeline_mode=once),
                  pl.BlockSpec(w_down.shape, full, pipeline_mode=once),
                  pl.BlockSpec((1, d), full)],
        out_specs=pl.BlockSpec((tm, d), row),
        out_shape=jax.ShapeDtypeStruct((t, d), F32),
        scratch_shapes=[pltpu.VMEM((HALO + tm, d), BF16),
                        pltpu.VMEM((2 * FFN_SPLIT * FFN_ROW_PARTS, HALO + tm // FFN_ROW_PARTS,
                                    hid // FFN_SPLIT), F32)],
        compiler_params=_cparams("parallel"),
        name="ffn",
    )(res, res, nw, w_up, cw, cb, w_down, fnw)


def kernel(x, mix_norm_w, w_in, gate_b, lambda_q1, lambda_k1, lambda_q2, lambda_k2, attn_subln_w,
           ssd_conv_w, ssd_conv_b, ssd_dt_bias, ssd_a_log, ssd_d, ssd_norm_w, w_attn_branch,
           w_ssd_branch, w_out, ffn_norm_w, w_up, ffn_conv_w, ffn_conv_b, w_down, final_norm_w):
    bsz, seq, d = x.shape
    t = bsz * seq
    depth = w_in.shape[0]
    att_w = ATT_HEADS * ATT_V_DIM
    inner = SSD_HEADS * SSD_HEAD_DIM
    cdim = inner + 2 * SSD_GROUPS * SSD_STATE
    c_q, c_k, c_v, c_z, c_x, c_dt = (att_w, 2 * att_w, 3 * att_w, 3 * att_w + inner,
                                     3 * att_w + inner + cdim, 3 * att_w + inner + cdim + SSD_HEADS)
    tk = min(ATT_KEYS, seq // 2)
    lc = min(SSD_TILE, seq)
    tm = min(ROW_TILE, seq)
    mm_rows = min(MM_ROWS, t)
    slopes = (jnp.exp2(-8.0 * jnp.arange(1, ATT_HEADS + 1, dtype=F32) / ATT_HEADS) * LOG2E).astype(F32)
    pad_heads = lambda v: jnp.pad(v.astype(F32), (0, LANES - SSD_HEADS)).reshape(1, LANES)

    res = x.reshape(t, d)
    for l in range(depth):
        lam_init = 0.8 - 0.6 * math.exp(-0.3 * l)
        wt = jnp.swapaxes(w_in[l], 0, 1).astype(BF16)
        wt_gate = wt[c_dt:]
        wdt = jnp.pad(wt[c_x:c_dt], ((0, LANES - SSD_HEADS), (0, 0)))

        h, dt_raw = _norm_dt(res, mix_norm_w[l].reshape(1, d), wdt, tm)
        blocks = lambda c0, c1: list(range(c0 // att_w, c1 // att_w))
        k = _matmul_nt(h, wt, blocks(c_q, c_k), att_w, mm_rows, "proj_k")
        z = _matmul_nt(h, wt, blocks(c_v, c_z), att_w, mm_rows, "proj_z")
        xbc = _matmul_nt(h, wt, blocks(c_z, c_x), att_w, mm_rows, "proj_xbc")
        gate_pre = _matmul_nt(h, wt_gate, blocks(0, wt_gate.shape[0]), att_w, mm_rows,
                              "proj_gate")
        qvt = _proj_t(wt, 0, c_k // att_w, att_w, ATT_HEAD_DIM ** -0.5 * LOG2E, h, bsz, seq, tk,
                      min(MM_ROWS, seq))

        lamv = jnp.stack([lambda_q1[l], lambda_k1[l], lambda_q2[l], lambda_k2[l]]).astype(F32)
        y_att = _attention(slopes, lamv, attn_subln_w[l].reshape(ATT_V_DIM, 1).astype(F32), qvt,
                           k.reshape(bsz, seq, att_w), bsz, seq, tk, lam_init)
        y_ssd = _ssd(z.reshape(bsz, seq, inner), xbc.reshape(bsz, seq, cdim),
                     dt_raw.reshape(bsz, seq, LANES), ssd_conv_w[l].astype(F32),
                     ssd_conv_b[l].reshape(1, cdim).astype(F32), pad_heads(ssd_dt_bias[l]),
                     pad_heads(ssd_a_log[l]),
                     jnp.repeat(ssd_d[l].astype(F32), SSD_HEAD_DIM).reshape(1, inner),
                     ssd_norm_w[l].reshape(1, inner).astype(F32), lc)
        res = _merge(res, y_att, y_ssd.reshape(t, inner), gate_pre,
                     gate_b[l].reshape(1, -1).astype(F32), w_attn_branch[l].astype(BF16),
                     w_ssd_branch[l].astype(BF16), w_out[l].astype(BF16), tm)
        res = _ffn(res, ffn_norm_w[l].reshape(1, d).astype(F32), w_up[l].astype(BF16),
                   ffn_conv_w[l].astype(F32), ffn_conv_b[l].reshape(1, -1).astype(F32),
                   w_down[l].astype(BF16), final_norm_w.reshape(1, d).astype(F32), seq, tm,
                   l == depth - 1)
    return res.reshape(bsz, seq, d)
```

```python
import functools
import math

import jax
import jax.numpy as jnp
from jax import lax
from jax.experimental import pallas as pl
from jax.experimental.pallas import tpu as pltpu

F32 = jnp.float32
BF16 = jnp.bfloat16

EPS = 1e-6
LOG2E = math.log2(math.e)
NEG_BIG = -1e30

ATT_HEADS = 8
ATT_HEAD_DIM = 64
ATT_V_DIM = 2 * ATT_HEAD_DIM
ATT_CHUNK = 64
SSD_HEADS = 32
SSD_HEAD_DIM = 64
SSD_GROUPS = 4
SSD_STATE = 128
SSD_CONV = 4
FFN_CONV = 3

LANES = 128
VMEM_LIMIT_BYTES = 56 * 1024 * 1024

ATT_KEYS = 256
ATT_HEADS_PER_STEP = 2
SSD_TILE = 128
ROW_TILE = 512
MM_ROWS = 1024
FFN_SPLIT = 1
FFN_ROW_PARTS = 1
HALO = 16


def _cparams(*sem):
    return pltpu.CompilerParams(dimension_semantics=sem, vmem_limit_bytes=VMEM_LIMIT_BYTES)


def _sigmoid(x):
    return 0.5 * (jnp.tanh(0.5 * x) + 1.0)


def _silu(x):
    h = 0.5 * x
    return h * (jnp.tanh(h) + 1.0)


def _rmsnorm(x, w):
    ms = jnp.mean(x * x, axis=-1, keepdims=True)
    return x * lax.rsqrt(ms + EPS) * w


def _split3(x):
    hi = x.astype(BF16)
    r1 = x - hi.astype(F32)
    mid = r1.astype(BF16)
    lo = (r1 - mid.astype(F32)).astype(BF16)
    return hi, mid, lo


_NT = (((1,), (1,)), ((), ()))


def _norm_dt_kernel(x_ref, w_ref, wdt_ref, wk_ref, h_ref, dt_ref, k_ref):
    h = _rmsnorm(x_ref[...], w_ref[...]).astype(BF16)
    h_ref[...] = h
    dt_ref[...] = lax.dot_general(h, wdt_ref[...], _NT, preferred_element_type=F32)
    k_ref[...] = lax.dot_general(h, wk_ref[...], _NT, preferred_element_type=F32).astype(BF16)


def _norm_dt(x, w, wdt, wt, k_block, blk, tm):
    t, d = x.shape
    return pl.pallas_call(
        _norm_dt_kernel,
        grid=(t // tm,),
        in_specs=[pl.BlockSpec((tm, d), lambda i: (i, 0)),
                  pl.BlockSpec((1, d), lambda i: (0, 0)),
                  pl.BlockSpec((LANES, d), lambda i: (0, 0)),
                  pl.BlockSpec((blk, d), lambda i: (k_block, 0))],
        out_specs=[pl.BlockSpec((tm, d), lambda i: (i, 0)),
                   pl.BlockSpec((tm, LANES), lambda i: (i, 0)),
                   pl.BlockSpec((tm, blk), lambda i: (i, 0))],
        out_shape=[jax.ShapeDtypeStruct((t, d), BF16),
                   jax.ShapeDtypeStruct((t, LANES), F32),
                   jax.ShapeDtypeStruct((t, blk), BF16)],
        compiler_params=_cparams("parallel"),
        name="norm_dt",
    )(x, w, wdt, wt)


def _mm_nt_kernel(a_ref, *refs):
    w_refs, o_ref = refs[:-1], refs[-1]
    a = a_ref[...]
    for j, w_ref in enumerate(w_refs):
        n = w_ref.shape[0]
        o_ref[:, j * n:(j + 1) * n] = lax.dot_general(
            a, w_ref[...], _NT, preferred_element_type=F32).astype(o_ref.dtype)


def _matmul_nt(a, wt, row_blocks, blk, tm, name):
    m, k = a.shape
    n = blk * len(row_blocks)
    return pl.pallas_call(
        _mm_nt_kernel,
        grid=(m // tm,),
        in_specs=[pl.BlockSpec((tm, k), lambda i: (i, 0))]
        + [pl.BlockSpec((blk, k), lambda i, r=r: (r, 0)) for r in row_blocks],
        out_specs=pl.BlockSpec((tm, n), lambda i: (i, 0)),
        out_shape=jax.ShapeDtypeStruct((m, n), BF16),
        compiler_params=_cparams("parallel"),
        name=name,
    )(a, *([wt] * len(row_blocks)))


def _proj_t_kernel(wq_ref, wv_ref, h_ref, o_ref, *, tq, q_scale):
    nq = wq_ref.shape[0]
    for s in range(o_ref.shape[1]):
        hs = h_ref[s * tq:(s + 1) * tq, :]
        qt = lax.dot_general(wq_ref[...], hs, _NT, preferred_element_type=F32)
        o_ref[0, s, 0:nq, :] = (qt * q_scale).astype(BF16)
        o_ref[0, s, nq:, :] = lax.dot_general(wv_ref[...], hs, _NT,
                                              preferred_element_type=F32).astype(BF16)


def _proj_t(wt, q_block, v_block, blk, q_scale, h, bsz, seq, tq, tm):
    d = wt.shape[1]
    nq = seq // tq
    per_step = tm // tq
    steps = seq // tm
    return pl.pallas_call(
        functools.partial(_proj_t_kernel, tq=tq, q_scale=q_scale),
        grid=(bsz, steps),
        in_specs=[pl.BlockSpec((blk, d), lambda b, i: (q_block, 0)),
                  pl.BlockSpec((blk, d), lambda b, i: (v_block, 0)),
                  pl.BlockSpec((tm, d), lambda b, i: (b * steps + i, 0))],
        out_specs=pl.BlockSpec((1, per_step, 2 * blk, tq), lambda b, i: (b, i, 0, 0)),
        out_shape=jax.ShapeDtypeStruct((bsz, nq, 2 * blk, tq), BF16),
        compiler_params=_cparams("parallel", "parallel"),
        name="proj_t",
    )(wt, wt, h)


def _attn_kernel(slopes_ref, lamv_ref, subw_ref, qt_ref, k_ref, vt_ref, o_ref, rhs_ref, kaug_ref,
                 s_ref, p_ref, acc_ref, stat_ref, dx_ref, *, tk, lam_init):
    nh = ATT_HEADS_PER_STEP
    tq = 2 * tk
    hp = pl.program_id(1)
    qi = pl.program_id(2)
    slope = [slopes_ref[hp * nh + hh] for hh in range(nh)]

    mx_i = 0
    head_0 = 4 * nh
    n_stat = 6
    al_i, m_i, l_i = 0, 2, 4

    row = lax.broadcasted_iota(jnp.int32, (ATT_V_DIM, tq), 0)
    ridx = lax.broadcasted_iota(jnp.int32, (8, tq), 0)
    rpos = lax.broadcasted_iota(jnp.int32, (8, tq), 1).astype(F32)
    for hh in range(nh):
        a = slope[hh]
        hsl = slice(hh * ATT_V_DIM, (hh + 1) * ATT_V_DIM)
        qt = jnp.concatenate([qt_ref[0, 0, hsl, :], qt_ref[0, 1, hsl, :]], axis=1)
        zero = jnp.zeros_like(qt)
        val = jnp.where(ridx < 3, a, -a * rpos)
        hi = val.astype(BF16).astype(F32)
        r1 = val - hi
        mid = r1.astype(BF16).astype(F32)
        lo = r1 - mid
        sel = ridx % 3
        piece = jnp.where(sel == 0, hi, jnp.where(sel == 1, mid, lo))
        piece = jnp.where(ridx < 6, piece, 0.0)
        aug = jnp.concatenate([piece, jnp.zeros((LANES - 8, tq), F32)], axis=0).astype(BF16)
        rhs_ref[2 * hh] = jnp.concatenate([jnp.where(row < ATT_HEAD_DIM, qt, zero), aug], axis=0)
        rhs_ref[2 * hh + 1] = jnp.concatenate([jnp.where(row >= ATT_HEAD_DIM, qt, zero), aug],
                                              axis=0)
        for mi in range(2):
            base = head_0 + hh * n_stat
            stat_ref[base + m_i + mi] = jnp.full((8, tq), NEG_BIG, F32)
            stat_ref[base + l_i + mi] = jnp.zeros((8, tq), F32)

    cpos = lax.broadcasted_iota(jnp.int32, (tk, LANES), 0).astype(F32)
    lane = lax.broadcasted_iota(jnp.int32, (tk, LANES), 1)
    kaug_ref[...] = jnp.where(lane < 3, cpos, jnp.where(lane < 6, 1.0, 0.0)).astype(BF16)
    acc_ref[...] = jnp.zeros_like(acc_ref)

    groups = [(hh, mi) for hh in range(nh) for mi in range(2)]

    def scores(kt, hh, mi, diag):
        hsl = slice(hh * ATT_V_DIM, (hh + 1) * ATT_V_DIM)
        for u in range(2):
            start = pl.multiple_of((2 * kt + u) * tk, tk)
            kf = jnp.concatenate([k_ref[0, pl.ds(start, tk), hsl], kaug_ref[...]], axis=1)
            s = jnp.dot(kf, rhs_ref[2 * hh + mi], preferred_element_type=F32)
            if diag:
                if u == 0:
                    s = jnp.concatenate([s[:, :tk] + dx_ref[hh], s[:, tk:]], axis=1)
                else:
                    s = jnp.concatenate([jnp.full((tk, tk), NEG_BIG, F32), s[:, tk:] + dx_ref[hh]],
                                        axis=1)
            sb = 4 * hh + 2 * mi + u
            s_ref[sb] = s
            stat_ref[mx_i + sb, 0:1, :] = jnp.max(s, axis=0, keepdims=True)

    def softmax(kt, hh, mi):
        base = head_0 + hh * n_stat
        cj = [-slope[hh] * (qi * tq - (2 * kt + u) * tk).astype(F32) for u in range(2)]
        sb = 4 * hh + 2 * mi
        m_old = stat_ref[base + m_i + mi, 0:1, :]
        mx = jnp.maximum(stat_ref[mx_i + sb, 0:1, :] + cj[0],
                         stat_ref[mx_i + sb + 1, 0:1, :] + cj[1])
        m_new = jnp.maximum(m_old, mx)
        alpha = jnp.exp2(m_old - m_new)
        lsum = alpha * stat_ref[base + l_i + mi, 0:1, :]
        for u in range(2):
            p = jnp.exp2(s_ref[sb + u] - (m_new - cj[u]))
            lsum = lsum + jnp.sum(p, axis=0, keepdims=True)
            p_ref[2 * hh + mi, u * tk:(u + 1) * tk, :] = p.astype(BF16)
        stat_ref[base + l_i + mi, 0:1, :] = lsum
        stat_ref[base + m_i + mi, 0:1, :] = m_new
        stat_ref[base + al_i + mi, 0:1, :] = alpha

    def values(kt, hh, mi):
        hsl = slice(hh * ATT_V_DIM, (hh + 1) * ATT_V_DIM)
        vt = jnp.concatenate([vt_ref[0, 2 * kt, hsl, :], vt_ref[0, 2 * kt + 1, hsl, :]], axis=1)
        acc_ref[2 * hh + mi] = (
            stat_ref[head_0 + hh * n_stat + al_i + mi, 0:1, :] * acc_ref[2 * hh + mi]
            + jnp.dot(vt, p_ref[2 * hh + mi], preferred_element_type=F32))

    ci = lax.broadcasted_iota(jnp.int32, (tk, tk), 0)
    ri = lax.broadcasted_iota(jnp.int32, (tk, tk), 1)
    for hh in range(nh):
        fix = jnp.where(ci > ri, (2.0 * slope[hh]) * (ri - ci).astype(F32), 0.0)
        dx_ref[hh] = jnp.where((ci // ATT_CHUNK) <= (ri // ATT_CHUNK), fix, NEG_BIG)
    for hh, mi in groups:
        scores(qi, hh, mi, True)

    @pl.when(qi >= 1)
    def _():
        for hh, mi in groups:
            softmax(qi, hh, mi)
        for hh, mi in groups:
            scores(0, hh, mi, False)

    def time_step(s):
        kv = jnp.where(s == 2, qi, s - 3)
        for hh, mi in groups:
            values(kv, hh, mi)
        for hh, mi in groups:
            softmax(s - 2, hh, mi)
        for hh, mi in groups:
            scores(s - 1, hh, mi, False)

    def body2(i, carry):
        time_step(2 * i + 2)
        time_step(2 * i + 3)
        return carry

    def body1(s, carry):
        time_step(s)
        return carry

    pairs = jnp.maximum(qi - 1, 0) // 2
    lax.fori_loop(0, pairs, body2, 0)
    lax.fori_loop(jnp.minimum(2 * pairs + 2, qi + 1), qi + 1, body1, 0)

    @pl.when(qi >= 1)
    def _():
        older = jnp.where(qi == 1, qi, qi - 2)
        for hh, mi in groups:
            values(older, hh, mi)

    last = jnp.where(qi == 0, qi, qi - 1)
    for hh, mi in groups:
        softmax(last, hh, mi)
    for hh, mi in groups:
        values(last, hh, mi)

    lv = lamv_ref[...]
    t1 = jnp.sum(lv[0:1] * lv[1:2], axis=1, keepdims=True)
    t2 = jnp.sum(lv[2:3] * lv[3:4], axis=1, keepdims=True)
    lam = jnp.exp(t1) - jnp.exp(t2) + lam_init
    subw = subw_ref[...] * (1.0 - lam_init)
    for hh in range(nh):
        base = head_0 + hh * n_stat
        o = (acc_ref[2 * hh] * (1.0 / stat_ref[base + l_i, 0:1, :])
             - lam * (acc_ref[2 * hh + 1] * (1.0 / stat_ref[base + l_i + 1, 0:1, :])))
        o = o * lax.rsqrt(jnp.mean(o * o, axis=0, keepdims=True) + EPS)
        o_ref[0, hh * ATT_V_DIM:(hh + 1) * ATT_V_DIM, :] = (o * subw).astype(BF16)


def _attention(slopes, lamv, subw, qvt, k, bsz, seq, tk, lam_init):
    nh = ATT_HEADS_PER_STEP
    tq = 2 * tk
    nq = seq // tq
    nk = seq // tk
    d = ATT_HEADS * ATT_V_DIM
    hw = nh * ATT_V_DIM
    return pl.pallas_call(
        functools.partial(_attn_kernel, tk=tk, lam_init=lam_init),
        grid=(bsz, ATT_HEADS // nh, nq),
        in_specs=[pl.BlockSpec(memory_space=pltpu.SMEM),
                  pl.BlockSpec((4, ATT_HEAD_DIM), lambda b, h, i: (0, 0)),
                  pl.BlockSpec((ATT_V_DIM, 1), lambda b, h, i: (0, 0)),
                  pl.BlockSpec((1, 2, hw, tk), lambda b, h, i: (b, i, h, 0)),
                  pl.BlockSpec((1, seq, hw), lambda b, h, i: (b, 0, h)),
                  pl.BlockSpec((1, nk, hw, tk), lambda b, h, i: (b, 0, ATT_HEADS // nh + h, 0))],
        out_specs=pl.BlockSpec((1, hw, tq), lambda b, h, i: (b, h, i)),
        out_shape=jax.ShapeDtypeStruct((bsz, d, seq), BF16),
        scratch_shapes=[pltpu.VMEM((2 * nh, 2 * LANES, tq), BF16),
                        pltpu.VMEM((tk, LANES), BF16),
                        pltpu.VMEM((4 * nh, tk, tq), F32),
                        pltpu.VMEM((2 * nh, tq, tq), BF16),
                        pltpu.VMEM((2 * nh, ATT_V_DIM, tq), F32),
                        pltpu.VMEM((10 * nh, 8, tq), F32),
                        pltpu.VMEM((nh, tk, tk), F32)],
        compiler_params=_cparams("parallel", "parallel", "arbitrary"),
        name="attn",
    )(slopes, lamv, subw, qvt, k, qvt)


def _ssd_kernel(z_ref, xbc_ref, dt_ref, cw_ref, cb_ref, dtb_ref, alog_ref, dexp_ref, nw_ref,
                y_ref, h_scr, xpad_scr, *, lc):
    inner = SSD_HEADS * SSD_HEAD_DIM
    gw = inner // SSD_GROUPS
    bc0 = inner
    cc0 = inner + SSD_GROUPS * SSD_STATE

    @pl.when(pl.program_id(1) == 0)
    def _():
        h_scr[...] = jnp.zeros_like(h_scr)
        xpad_scr[0:HALO, :] = jnp.zeros((HALO, xpad_scr.shape[1]), BF16)

    xraw = xbc_ref[0]
    xpad_scr[HALO:, :] = xraw
    xpad = xpad_scr[...]
    cw = cw_ref[...]
    conv = cb_ref[...] + cw[SSD_CONV - 1:SSD_CONV] * xraw.astype(F32)
    srow = lax.broadcasted_iota(jnp.int32, (lc, HALO + lc), 0)
    scol = lax.broadcasted_iota(jnp.int32, (lc, HALO + lc), 1)
    for kk in range(SSD_CONV - 1):
        shift = jnp.where(scol == srow + (HALO - (SSD_CONV - 1) + kk), 1.0, 0.0).astype(BF16)
        conv = conv + cw[kk:kk + 1] * jnp.dot(shift, xpad, preferred_element_type=F32)
    xpad_scr[0:HALO, :] = xraw[lc - HALO:, :]
    xc = _silu(conv)
    xs = xc[:, :inner]

    x = dt_ref[0] + dtb_ref[...]
    dt = jnp.maximum(x, 0.0) + jnp.log(1.0 + jnp.exp(-jnp.abs(x)))
    a = dt * (-jnp.exp(alog_ref[...]))

    ii = lax.broadcasted_iota(jnp.int32, (lc, lc), 0)
    jj = lax.broadcasted_iota(jnp.int32, (lc, lc), 1)
    tril = jj <= ii
    tri = jnp.where(tril, 1.0, 0.0).astype(BF16)
    a3 = jnp.concatenate(_split3(a), axis=1)
    c3 = jnp.dot(tri, a3, preferred_element_type=F32)
    acum = c3[:, :LANES] + c3[:, LANES:2 * LANES] + c3[:, 2 * LANES:]

    er = lax.broadcasted_iota(jnp.int32, (LANES, inner), 0)
    ec = lax.broadcasted_iota(jnp.int32, (LANES, inner), 1)
    expand = jnp.where(ec // SSD_HEAD_DIM == er, 1.0, 0.0).astype(BF16)
    dt_hi = dt.astype(BF16)
    dt_lo = (dt - dt_hi.astype(F32)).astype(BF16)
    stack = jnp.concatenate(list(_split3(acum)) + [dt_hi, dt_lo], axis=0)
    ex = jnp.dot(stack, expand, preferred_element_type=F32)
    acum_e = ex[:lc] + ex[lc:2 * lc] + ex[2 * lc:3 * lc]
    dt_e = ex[3 * lc:4 * lc] + ex[4 * lc:]

    acum_last = acum_e[lc - 1:lc, :]
    xd = xs * dt_e
    xd_b = xd.astype(BF16)
    xd_end = (xd * jnp.exp(acum_last - acum_e)).astype(BF16)
    dec_in = jnp.exp(acum_e)
    dec_all = jnp.exp(acum_last)
    acum_t = acum.T

    lane = lax.broadcasted_iota(jnp.int32, (lc, LANES), 1)
    y_parts = []
    for g in range(SSD_GROUPS):
        bg = xc[:, bc0 + g * SSD_STATE:bc0 + (g + 1) * SSD_STATE]
        cg = xc[:, cc0 + g * SSD_STATE:cc0 + (g + 1) * SSD_STATE].astype(BF16)
        gsl = slice(g * gw, (g + 1) * gw)
        cbt = lax.dot_general(cg, bg.astype(BF16), (((1,), (1,)), ((), ())),
                              preferred_element_type=F32)
        h_prev = h_scr[:, gsl]
        y_off = jnp.dot(cg, h_prev.astype(BF16), preferred_element_type=F32) * dec_in[:, gsl]
        st = jnp.dot(bg.T.astype(BF16), xd_end[:, gsl], preferred_element_type=F32)
        h_scr[:, gsl] = h_prev * dec_all[:, gsl] + st
        heads_per_group = SSD_HEADS // SSD_GROUPS
        for pr in range(heads_per_group // 2):
            r0 = g * heads_per_group + 2 * pr
            ms = []
            for r in (r0, r0 + 1):
                seg = acum[:, r:r + 1] - acum_t[r:r + 1, :]
                ms.append((cbt * jnp.exp(jnp.where(tril, seg, NEG_BIG))).astype(BF16))
            xp = xd_b[:, r0 * SSD_HEAD_DIM:r0 * SSD_HEAD_DIM + LANES]
            zero = jnp.zeros_like(xp)
            xblk = jnp.concatenate([jnp.where(lane < SSD_HEAD_DIM, xp, zero),
                                    jnp.where(lane >= SSD_HEAD_DIM, xp, zero)], axis=0)
            y_diag = jnp.dot(jnp.concatenate(ms, axis=1), xblk, preferred_element_type=F32)
            lsl = slice(pr * LANES, (pr + 1) * LANES)
            y_parts.append(y_diag + y_off[:, lsl])
    y = jnp.concatenate(y_parts, axis=1) + xs * dexp_ref[...]

    gated = y * _silu(z_ref[0].astype(F32))
    outs = []
    for g in range(SSD_GROUPS):
        gg = gated[:, g * gw:(g + 1) * gw]
        outs.append(gg * lax.rsqrt(jnp.mean(gg * gg, axis=-1, keepdims=True) + EPS))
    y_ref[0] = (jnp.concatenate(outs, axis=1) * nw_ref[...]).astype(BF16)


def _ssd(z, xbc, dt, cw, cb, dtb, alog, dexp, nw, lc):
    bsz, seq, inner = z.shape
    cdim = xbc.shape[-1]
    full = lambda b, c: (0, 0)
    return pl.pallas_call(
        functools.partial(_ssd_kernel, lc=lc),
        grid=(bsz, seq // lc),
        in_specs=[pl.BlockSpec((1, lc, inner), lambda b, c: (b, c, 0)),
                  pl.BlockSpec((1, lc, cdim), lambda b, c: (b, c, 0)),
                  pl.BlockSpec((1, lc, LANES), lambda b, c: (b, c, 0)),
                  pl.BlockSpec((SSD_CONV, cdim), full),
                  pl.BlockSpec((1, cdim), full),
                  pl.BlockSpec((1, LANES), full),
                  pl.BlockSpec((1, LANES), full),
                  pl.BlockSpec((1, inner), full),
                  pl.BlockSpec((1, inner), full)],
        out_specs=pl.BlockSpec((1, lc, inner), lambda b, c: (b, c, 0)),
        out_shape=jax.ShapeDtypeStruct((bsz, seq, inner), BF16),
        scratch_shapes=[pltpu.VMEM((SSD_STATE, inner), F32), pltpu.VMEM((HALO + lc, cdim), BF16)],
        compiler_params=_cparams("parallel", "arbitrary"),
        name="ssd",
    )(z, xbc, dt, cw, cb, dtb, alog, dexp, nw)


def _merge_kernel(x_ref, ya_ref, ys_ref, gp_ref, gb_ref, wa_ref, ws_ref, wo_ref, o_ref):
    d = x_ref.shape[-1]
    gates = _sigmoid(gp_ref[...].astype(F32) + gb_ref[...])
    pa = lax.dot_general(ya_ref[0], wa_ref[...], (((0,), (0,)), ((), ())),
                         preferred_element_type=F32)
    ps = jnp.dot(ys_ref[...], ws_ref[...], preferred_element_type=F32)
    merged = gates[:, :d] * pa + gates[:, d:] * ps
    o_ref[...] = x_ref[...] + jnp.dot(merged.astype(BF16), wo_ref[...], preferred_element_type=F32)


def _merge(x, ya, ys, gp, gb, wa, ws, wo, tm):
    t, d = x.shape
    tiles_per_seq = ya.shape[2] // tm
    row = lambda i: (i, 0)
    full = lambda i: (0, 0)
    return pl.pallas_call(
        _merge_kernel,
        grid=(t // tm,),
        in_specs=[pl.BlockSpec((tm, d), row),
                  pl.BlockSpec((1, ya.shape[1], tm),
                               lambda i: (i // tiles_per_seq, 0, i % tiles_per_seq)),
                  pl.BlockSpec((tm, ys.shape[1]), row),
                  pl.BlockSpec((tm, gp.shape[1]), row),
                  pl.BlockSpec(gb.shape, full),
                  pl.BlockSpec(wa.shape, full),
                  pl.BlockSpec(ws.shape, full),
                  pl.BlockSpec(wo.shape, full)],
        out_specs=pl.BlockSpec((tm, d), row),
        out_shape=jax.ShapeDtypeStruct((t, d), F32),
        compiler_params=_cparams("parallel"),
        name="merge",
    )(x, ya, ys, gp, gb, wa, ws, wo)


def _ffn_kernel(res_ref, halo_ref, nw_ref, wup_ref, cw_ref, cb_ref, wd_ref, fnw_ref, o_ref, h_scr,
                u_scr, *, tm, hid, tiles_per_seq, final_norm):
    i = pl.program_id(0)
    nw = nw_ref[...]
    hh = _rmsnorm(halo_ref[...], nw)
    hh = jnp.where(i % tiles_per_seq == 0, 0.0, hh)
    h_scr[0:HALO, :] = hh.astype(BF16)
    h_scr[HALO:, :] = _rmsnorm(res_ref[...], nw).astype(BF16)
    th = hid // FFN_SPLIT
    rows = tm // FFN_ROW_PARTS

    for part in range(FFN_ROW_PARTS):
        r0 = part * rows
        hx = h_scr[r0:r0 + HALO + rows, :]

        def branch(c0, slot):
            u_scr[slot] = jnp.dot(hx, wup_ref[:, c0:c0 + th], preferred_element_type=F32)
            cw = cw_ref[:, c0:c0 + th]
            out = cb_ref[:, c0:c0 + th]
            for kk in range(FFN_CONV):
                off = HALO - (FFN_CONV - 1) + kk
                out = out + cw[kk:kk + 1] * u_scr[slot, off:off + rows, :]
            return out

        total = res_ref[r0:r0 + rows, :]
        for c in range(FFN_SPLIT):
            slot = 2 * (part * FFN_SPLIT + c)
            act = (_silu(branch(hid + c * th, slot)) * branch(c * th, slot + 1)).astype(BF16)
            total = total + jnp.dot(act, wd_ref[c * th:(c + 1) * th, :],
                                    preferred_element_type=F32)
        o_ref[r0:r0 + rows, :] = _rmsnorm(total, fnw_ref[...]) if final_norm else total


def _ffn(res, nw, w_up, cw, cb, w_down, fnw, seq, tm, final_norm):
    t, d = res.shape
    hid = w_down.shape[0]
    row = lambda i: (i, 0)
    full = lambda i: (0, 0)
    once = pl.Buffered(1)
    halo_blocks = tm // HALO
    return pl.pallas_call(
        functools.partial(_ffn_kernel, tm=tm, hid=hid, tiles_per_seq=seq // tm,
                          final_norm=final_norm),
        grid=(t // tm,),
        in_specs=[pl.BlockSpec((tm, d), row),
                  pl.BlockSpec((HALO, d), lambda i: (jnp.maximum(i * halo_blocks - 1, 0), 0)),
                  pl.BlockSpec((1, d), full),
                  pl.BlockSpec(w_up.shape, full, pipeline_mode=once),
                  pl.BlockSpec(cw.shape, full, pipeline_mode=once),
                  pl.BlockSpec(cb.shape, full, pipeline_mode=once),
                  pl.BlockSpec(w_down.shape, full, pipeline_mode=once),
                  pl.BlockSpec((1, d), full)],
        out_specs=pl.BlockSpec((tm, d), row),
        out_shape=jax.ShapeDtypeStruct((t, d), F32),
        scratch_shapes=[pltpu.VMEM((HALO + tm, d), BF16),
                        pltpu.VMEM((2 * FFN_SPLIT * FFN_ROW_PARTS, HALO + tm // FFN_ROW_PARTS,
                                    hid // FFN_SPLIT), F32)],
        compiler_params=_cparams("parallel"),
        name="ffn",
    )(res, res, nw, w_up, cw, cb, w_down, fnw)


def kernel(x, mix_norm_w, w_in, gate_b, lambda_q1, lambda_k1, lambda_q2, lambda_k2, attn_subln_w,
           ssd_conv_w, ssd_conv_b, ssd_dt_bias, ssd_a_log, ssd_d, ssd_norm_w, w_attn_branch,
           w_ssd_branch, w_out, ffn_norm_w, w_up, ffn_conv_w, ffn_conv_b, w_down, final_norm_w):
    bsz, seq, d = x.shape
    t = bsz * seq
    depth = w_in.shape[0]
    att_w = ATT_HEADS * ATT_V_DIM
    inner = SSD_HEADS * SSD_HEAD_DIM
    cdim = inner + 2 * SSD_GROUPS * SSD_STATE
    c_q, c_k, c_v, c_z, c_x, c_dt = (att_w, 2 * att_w, 3 * att_w, 3 * att_w + inner,
                                     3 * att_w + inner + cdim, 3 * att_w + inner + cdim + SSD_HEADS)
    tk = min(ATT_KEYS, seq // 2)
    lc = min(SSD_TILE, seq)
    tm = min(ROW_TILE, seq)
    mm_rows = min(MM_ROWS, t)
    slopes = (jnp.exp2(-8.0 * jnp.arange(1, ATT_HEADS + 1, dtype=F32) / ATT_HEADS) * LOG2E).astype(F32)
    pad_heads = lambda v: jnp.pad(v.astype(F32), (0, LANES - SSD_HEADS)).reshape(1, LANES)

    res = x.reshape(t, d)
    for l in range(depth):
        lam_init = 0.8 - 0.6 * math.exp(-0.3 * l)
        wt = jnp.swapaxes(w_in[l], 0, 1).astype(BF16)
        wt_gate = wt[c_dt:]
        wdt = jnp.pad(wt[c_x:c_dt], ((0, LANES - SSD_HEADS), (0, 0)))

        h, dt_raw, k = _norm_dt(res, mix_norm_w[l].reshape(1, d), wdt, wt, c_q // att_w, att_w,
                                mm_rows)
        blocks = lambda c0, c1: list(range(c0 // att_w, c1 // att_w))
        z = _matmul_nt(h, wt, blocks(c_v, c_z), att_w, mm_rows, "proj_z")
        xbc = _matmul_nt(h, wt, blocks(c_z, c_x), att_w, mm_rows, "proj_xbc")
        gate_pre = _matmul_nt(h, wt_gate, blocks(0, wt_gate.shape[0]), att_w, mm_rows,
                              "proj_gate")
        qvt = _proj_t(wt, 0, c_k // att_w, att_w, ATT_HEAD_DIM ** -0.5 * LOG2E, h, bsz, seq, tk,
                      min(MM_ROWS, seq))

        lamv = jnp.stack([lambda_q1[l], lambda_k1[l], lambda_q2[l], lambda_k2[l]]).astype(F32)
        y_att = _attention(slopes, lamv, attn_subln_w[l].reshape(ATT_V_DIM, 1).astype(F32), qvt,
                           k.reshape(bsz, seq, att_w), bsz, seq, tk, lam_init)
        y_ssd = _ssd(z.reshape(bsz, seq, inner), xbc.reshape(bsz, seq, cdim),
                     dt_raw.reshape(bsz, seq, LANES), ssd_conv_w[l].astype(F32),
                     ssd_conv_b[l].reshape(1, cdim).astype(F32), pad_heads(ssd_dt_bias[l]),
                     pad_heads(ssd_a_log[l]),
                     jnp.repeat(ssd_d[l].astype(F32), SSD_HEAD_DIM).reshape(1, inner),
                     ssd_norm_w[l].reshape(1, inner).astype(F32), lc)
        res = _merge(res, y_att, y_ssd.reshape(t, inner), gate_pre,
                     gate_b[l].reshape(1, -1).astype(F32), w_attn_branch[l].astype(BF16),
                     w_ssd_branch[l].astype(BF16), w_out[l].astype(BF16), tm)
        res = _ffn(res, ffn_norm_w[l].reshape(1, d).astype(F32), w_up[l].astype(BF16),
                   ffn_conv_w[l].astype(F32), ffn_conv_b[l].reshape(1, -1).astype(F32),
                   w_down[l].astype(BF16), final_norm_w.reshape(1, d).astype(F32), seq, tm,
                   l == depth - 1)
    return res.reshape(bsz, seq, d)
```

```python
import functools
import math

import jax
import jax.numpy as jnp
from jax import lax
from jax.experimental import pallas as pl
from jax.experimental.pallas import tpu as pltpu

F32 = jnp.float32
BF16 = jnp.bfloat16

EPS = 1e-6
LOG2E = math.log2(math.e)
NEG_BIG = -1e30

ATT_HEADS = 8
ATT_HEAD_DIM = 64
ATT_V_DIM = 2 * ATT_HEAD_DIM
ATT_CHUNK = 64
SSD_HEADS = 32
SSD_HEAD_DIM = 64
SSD_GROUPS = 4
SSD_STATE = 128
SSD_CONV = 4
FFN_CONV = 3

LANES = 128
VMEM_LIMIT_BYTES = 56 * 1024 * 1024

ATT_KEYS = 256
ATT_HEADS_PER_STEP = 2
SSD_TILE = 128
ROW_TILE = 512
MM_ROWS = 1024
FFN_SPLIT = 1
FFN_ROW_PARTS = 1
HALO = 16


def _cparams(*sem):
    return pltpu.CompilerParams(dimension_semantics=sem, vmem_limit_bytes=VMEM_LIMIT_BYTES)


def _sigmoid(x):
    return 0.5 * (jnp.tanh(0.5 * x) + 1.0)


def _silu(x):
    h = 0.5 * x
    return h * (jnp.tanh(h) + 1.0)


def _rmsnorm(x, w):
    ms = jnp.mean(x * x, axis=-1, keepdims=True)
    return x * lax.rsqrt(ms + EPS) * w


def _split3(x):
    hi = x.astype(BF16)
    r1 = x - hi.astype(F32)
    mid = r1.astype(BF16)
    lo = (r1 - mid.astype(F32)).astype(BF16)
    return hi, mid, lo


_NT = (((1,), (1,)), ((), ()))


def _norm_dt_kernel(x_ref, w_ref, wdt_ref, wk_ref, h_ref, dt_ref, k_ref):
    h = _rmsnorm(x_ref[...], w_ref[...]).astype(BF16)
    h_ref[...] = h
    dt_ref[...] = lax.dot_general(h, wdt_ref[...], _NT, preferred_element_type=F32)
    k_ref[...] = lax.dot_general(h, wk_ref[...], _NT, preferred_element_type=F32).astype(BF16)


def _norm_dt(x, w, wdt, wt, k_block, blk, tm):
    t, d = x.shape
    return pl.pallas_call(
        _norm_dt_kernel,
        grid=(t // tm,),
        in_specs=[pl.BlockSpec((tm, d), lambda i: (i, 0)),
                  pl.BlockSpec((1, d), lambda i: (0, 0)),
                  pl.BlockSpec((LANES, d), lambda i: (0, 0)),
                  pl.BlockSpec((blk, d), lambda i: (k_block, 0))],
        out_specs=[pl.BlockSpec((tm, d), lambda i: (i, 0)),
                   pl.BlockSpec((tm, LANES), lambda i: (i, 0)),
                   pl.BlockSpec((tm, blk), lambda i: (i, 0))],
        out_shape=[jax.ShapeDtypeStruct((t, d), BF16),
                   jax.ShapeDtypeStruct((t, LANES), F32),
                   jax.ShapeDtypeStruct((t, blk), BF16)],
        compiler_params=_cparams("parallel"),
        name="norm_dt",
    )(x, w, wdt, wt)


def _mm_nt_kernel(a_ref, *refs):
    w_refs, o_ref = refs[:-1], refs[-1]
    a = a_ref[...]
    for j, w_ref in enumerate(w_refs):
        n = w_ref.shape[0]
        o_ref[:, j * n:(j + 1) * n] = lax.dot_general(
            a, w_ref[...], _NT, preferred_element_type=F32).astype(o_ref.dtype)


def _matmul_nt(a, wt, row_blocks, blk, tm, name):
    m, k = a.shape
    n = blk * len(row_blocks)
    return pl.pallas_call(
        _mm_nt_kernel,
        grid=(m // tm,),
        in_specs=[pl.BlockSpec((tm, k), lambda i: (i, 0))]
        + [pl.BlockSpec((blk, k), lambda i, r=r: (r, 0)) for r in row_blocks],
        out_specs=pl.BlockSpec((tm, n), lambda i: (i, 0)),
        out_shape=jax.ShapeDtypeStruct((m, n), BF16),
        compiler_params=_cparams("parallel"),
        name=name,
    )(a, *([wt] * len(row_blocks)))


def _proj_t_kernel(wq_ref, wv_ref, h_ref, o_ref, *, tq, q_scale):
    nq = wq_ref.shape[0]
    for s in range(o_ref.shape[1]):
        hs = h_ref[s * tq:(s + 1) * tq, :]
        qt = lax.dot_general(wq_ref[...], hs, _NT, preferred_element_type=F32)
        o_ref[0, s, 0:nq, :] = (qt * q_scale).astype(BF16)
        o_ref[0, s, nq:, :] = lax.dot_general(wv_ref[...], hs, _NT,
                                              preferred_element_type=F32).astype(BF16)


def _proj_t(wt, q_block, v_block, blk, q_scale, h, bsz, seq, tq, tm):
    d = wt.shape[1]
    nq = seq // tq
    per_step = tm // tq
    steps = seq // tm
    return pl.pallas_call(
        functools.partial(_proj_t_kernel, tq=tq, q_scale=q_scale),
        grid=(bsz, steps),
        in_specs=[pl.BlockSpec((blk, d), lambda b, i: (q_block, 0)),
                  pl.BlockSpec((blk, d), lambda b, i: (v_block, 0)),
                  pl.BlockSpec((tm, d), lambda b, i: (b * steps + i, 0))],
        out_specs=pl.BlockSpec((1, per_step, 2 * blk, tq), lambda b, i: (b, i, 0, 0)),
        out_shape=jax.ShapeDtypeStruct((bsz, nq, 2 * blk, tq), BF16),
        compiler_params=_cparams("parallel", "parallel"),
        name="proj_t",
    )(wt, wt, h)


def _attn_kernel(slopes_ref, lamv_ref, subw_ref, qt_ref, k_ref, vt_ref, o_ref, rhs_ref, kaug_ref,
                 s_ref, p_ref, acc_ref, stat_ref, dx_ref, *, tk, lam_init):
    nh = ATT_HEADS_PER_STEP
    tq = 2 * tk
    hp = pl.program_id(1)
    qi = pl.program_id(2)
    slope = [slopes_ref[hp * nh + hh] for hh in range(nh)]

    mx_i = 0
    head_0 = 4 * nh
    n_stat = 6
    al_i, m_i, l_i = 0, 2, 4

    row = lax.broadcasted_iota(jnp.int32, (ATT_V_DIM, tq), 0)
    ridx = lax.broadcasted_iota(jnp.int32, (8, tq), 0)
    rpos = lax.broadcasted_iota(jnp.int32, (8, tq), 1).astype(F32)
    for hh in range(nh):
        a = slope[hh]
        hsl = slice(hh * ATT_V_DIM, (hh + 1) * ATT_V_DIM)
        qt = jnp.concatenate([qt_ref[0, 0, hsl, :], qt_ref[0, 1, hsl, :]], axis=1)
        zero = jnp.zeros_like(qt)
        val = jnp.where(ridx < 3, a, -a * rpos)
        hi = val.astype(BF16).astype(F32)
        r1 = val - hi
        mid = r1.astype(BF16).astype(F32)
        lo = r1 - mid
        sel = ridx % 3
        piece = jnp.where(sel == 0, hi, jnp.where(sel == 1, mid, lo))
        piece = jnp.where(ridx < 6, piece, 0.0)
        aug = jnp.concatenate([piece, jnp.zeros((LANES - 8, tq), F32)], axis=0).astype(BF16)
        rhs_ref[2 * hh] = jnp.concatenate([jnp.where(row < ATT_HEAD_DIM, qt, zero), aug], axis=0)
        rhs_ref[2 * hh + 1] = jnp.concatenate([jnp.where(row >= ATT_HEAD_DIM, qt, zero), aug],
                                              axis=0)
        for mi in range(2):
            base = head_0 + hh * n_stat
            stat_ref[base + m_i + mi] = jnp.full((8, tq), NEG_BIG, F32)
            stat_ref[base + l_i + mi] = jnp.zeros((8, tq), F32)

    cpos = lax.broadcasted_iota(jnp.int32, (tk, LANES), 0).astype(F32)
    lane = lax.broadcasted_iota(jnp.int32, (tk, LANES), 1)
    kaug_ref[...] = jnp.where(lane < 3, cpos, jnp.where(lane < 6, 1.0, 0.0)).astype(BF16)
    acc_ref[...] = jnp.zeros_like(acc_ref)

    groups = [(hh, mi) for hh in range(nh) for mi in range(2)]

    def scores(kt, hh, mi, diag):
        hsl = slice(hh * ATT_V_DIM, (hh + 1) * ATT_V_DIM)
        for u in range(2):
            start = pl.multiple_of((2 * kt + u) * tk, tk)
            kf = jnp.concatenate([k_ref[0, pl.ds(start, tk), hsl], kaug_ref[...]], axis=1)
            s = jnp.dot(kf, rhs_ref[2 * hh + mi], preferred_element_type=F32)
            if diag:
                if u == 0:
                    s = jnp.concatenate([s[:, :tk] + dx_ref[hh], s[:, tk:]], axis=1)
                else:
                    s = jnp.concatenate([jnp.full((tk, tk), NEG_BIG, F32), s[:, tk:] + dx_ref[hh]],
                                        axis=1)
            sb = 4 * hh + 2 * mi + u
            s_ref[sb] = s
            stat_ref[mx_i + sb, 0:1, :] = jnp.max(s, axis=0, keepdims=True)

    def softmax(kt, hh, mi):
        base = head_0 + hh * n_stat
        cj = [-slope[hh] * (qi * tq - (2 * kt + u) * tk).astype(F32) for u in range(2)]
        sb = 4 * hh + 2 * mi
        m_old = stat_ref[base + m_i + mi, 0:1, :]
        mx = jnp.maximum(stat_ref[mx_i + sb, 0:1, :] + cj[0],
                         stat_ref[mx_i + sb + 1, 0:1, :] + cj[1])
        m_new = jnp.maximum(m_old, mx)
        alpha = jnp.exp2(m_old - m_new)
        lsum = alpha * stat_ref[base + l_i + mi, 0:1, :]
        for u in range(2):
            p = jnp.exp2(s_ref[sb + u] - (m_new - cj[u]))
            lsum = lsum + jnp.sum(p, axis=0, keepdims=True)
            p_ref[2 * hh + mi, u * tk:(u + 1) * tk, :] = p.astype(BF16)
        stat_ref[base + l_i + mi, 0:1, :] = lsum
        stat_ref[base + m_i + mi, 0:1, :] = m_new
        stat_ref[base + al_i + mi, 0:1, :] = alpha

    def values(kt, hh, mi):
        hsl = slice(hh * ATT_V_DIM, (hh + 1) * ATT_V_DIM)
        vt = jnp.concatenate([vt_ref[0, 2 * kt, hsl, :], vt_ref[0, 2 * kt + 1, hsl, :]], axis=1)
        acc_ref[2 * hh + mi] = (
            stat_ref[head_0 + hh * n_stat + al_i + mi, 0:1, :] * acc_ref[2 * hh + mi]
            + jnp.dot(vt, p_ref[2 * hh + mi], preferred_element_type=F32))

    ci = lax.broadcasted_iota(jnp.int32, (tk, tk), 0)
    ri = lax.broadcasted_iota(jnp.int32, (tk, tk), 1)
    for hh in range(nh):
        fix = jnp.where(ci > ri, (2.0 * slope[hh]) * (ri - ci).astype(F32), 0.0)
        dx_ref[hh] = jnp.where((ci // ATT_CHUNK) <= (ri // ATT_CHUNK), fix, NEG_BIG)
    for hh, mi in groups:
        scores(qi, hh, mi, True)

    for hh, mi in groups:
        softmax(qi, hh, mi)
    for hh, mi in groups:
        scores(0, hh, mi, False)

    def time_step(s):
        kv = jnp.where(s == 2, qi, s - 3)
        for hh, mi in groups:
            values(kv, hh, mi)
        for hh, mi in groups:
            softmax(s - 2, hh, mi)
        for hh, mi in groups:
            scores(s - 1, hh, mi, False)

    def body2(i, carry):
        time_step(2 * i + 2)
        time_step(2 * i + 3)
        return carry

    def body1(s, carry):
        time_step(s)
        return carry

    pairs = jnp.maximum(qi - 1, 0) // 2
    lax.fori_loop(0, pairs, body2, 0)
    lax.fori_loop(jnp.minimum(2 * pairs + 2, qi + 1), qi + 1, body1, 0)

    older = jnp.where(qi <= 1, qi, qi - 2)
    for hh, mi in groups:
        values(older, hh, mi)

    @pl.when(qi >= 1)
    def _():
        for hh, mi in groups:
            softmax(qi - 1, hh, mi)
        for hh, mi in groups:
            values(qi - 1, hh, mi)

    lv = lamv_ref[...]
    t1 = jnp.sum(lv[0:1] * lv[1:2], axis=1, keepdims=True)
    t2 = jnp.sum(lv[2:3] * lv[3:4], axis=1, keepdims=True)
    lam = jnp.exp(t1) - jnp.exp(t2) + lam_init
    subw = subw_ref[...] * (1.0 - lam_init)
    for hh in range(nh):
        base = head_0 + hh * n_stat
        o = (acc_ref[2 * hh] * (1.0 / stat_ref[base + l_i, 0:1, :])
             - lam * (acc_ref[2 * hh + 1] * (1.0 / stat_ref[base + l_i + 1, 0:1, :])))
        o = o * lax.rsqrt(jnp.mean(o * o, axis=0, keepdims=True) + EPS)
        o_ref[0, hh * ATT_V_DIM:(hh + 1) * ATT_V_DIM, :] = (o * subw).astype(BF16)


def _attention(slopes, lamv, subw, qvt, k, bsz, seq, tk, lam_init):
    nh = ATT_HEADS_PER_STEP
    tq = 2 * tk
    nq = seq // tq
    nk = seq // tk
    d = ATT_HEADS * ATT_V_DIM
    hw = nh * ATT_V_DIM
    return pl.pallas_call(
        functools.partial(_attn_kernel, tk=tk, lam_init=lam_init),
        grid=(bsz, ATT_HEADS // nh, nq),
        in_specs=[pl.BlockSpec(memory_space=pltpu.SMEM),
                  pl.BlockSpec((4, ATT_HEAD_DIM), lambda b, h, i: (0, 0)),
                  pl.BlockSpec((ATT_V_DIM, 1), lambda b, h, i: (0, 0)),
                  pl.BlockSpec((1, 2, hw, tk), lambda b, h, i: (b, i, h, 0)),
                  pl.BlockSpec((1, seq, hw), lambda b, h, i: (b, 0, h)),
                  pl.BlockSpec((1, nk, hw, tk), lambda b, h, i: (b, 0, ATT_HEADS // nh + h, 0))],
        out_specs=pl.BlockSpec((1, hw, tq), lambda b, h, i: (b, h, i)),
        out_shape=jax.ShapeDtypeStruct((bsz, d, seq), BF16),
        scratch_shapes=[pltpu.VMEM((2 * nh, 2 * LANES, tq), BF16),
                        pltpu.VMEM((tk, LANES), BF16),
                        pltpu.VMEM((4 * nh, tk, tq), F32),
                        pltpu.VMEM((2 * nh, tq, tq), BF16),
                        pltpu.VMEM((2 * nh, ATT_V_DIM, tq), F32),
                        pltpu.VMEM((10 * nh, 8, tq), F32),
                        pltpu.VMEM((nh, tk, tk), F32)],
        compiler_params=_cparams("parallel", "parallel", "arbitrary"),
        name="attn",
    )(slopes, lamv, subw, qvt, k, qvt)


def _ssd_kernel(z_ref, xbc_ref, dt_ref, cw_ref, cb_ref, dtb_ref, alog_ref, dexp_ref, nw_ref,
                y_ref, h_scr, xpad_scr, *, lc):
    first = pl.program_id(0) == 0
    for b in range(z_ref.shape[0]):
        _ssd_sequence_step(first, z_ref.at[b], xbc_ref.at[b], dt_ref.at[b], cw_ref, cb_ref,
                           dtb_ref, alog_ref, dexp_ref, nw_ref, y_ref.at[b], h_scr.at[b],
                           xpad_scr.at[b], lc=lc)


def _ssd_sequence_step(first, z_ref, xbc_ref, dt_ref, cw_ref, cb_ref, dtb_ref, alog_ref, dexp_ref,
                       nw_ref, y_ref, h_scr, xpad_scr, *, lc):
    inner = SSD_HEADS * SSD_HEAD_DIM
    gw = inner // SSD_GROUPS
    bc0 = inner
    cc0 = inner + SSD_GROUPS * SSD_STATE

    @pl.when(first)
    def _():
        h_scr[...] = jnp.zeros_like(h_scr)
        xpad_scr[0:HALO, :] = jnp.zeros((HALO, xpad_scr.shape[1]), BF16)

    xraw = xbc_ref[...]
    xpad_scr[HALO:, :] = xraw
    xpad = xpad_scr[...]
    cw = cw_ref[...]
    conv = cb_ref[...] + cw[SSD_CONV - 1:SSD_CONV] * xraw.astype(F32)
    srow = lax.broadcasted_iota(jnp.int32, (lc, HALO + lc), 0)
    scol = lax.broadcasted_iota(jnp.int32, (lc, HALO + lc), 1)
    for kk in range(SSD_CONV - 1):
        shift = jnp.where(scol == srow + (HALO - (SSD_CONV - 1) + kk), 1.0, 0.0).astype(BF16)
        conv = conv + cw[kk:kk + 1] * jnp.dot(shift, xpad, preferred_element_type=F32)
    xpad_scr[0:HALO, :] = xraw[lc - HALO:, :]
    xc = _silu(conv)
    xs = xc[:, :inner]

    x = dt_ref[...] + dtb_ref[...]
    dt = jnp.maximum(x, 0.0) + jnp.log(1.0 + jnp.exp(-jnp.abs(x)))
    a = dt * (-jnp.exp(alog_ref[...]))

    ii = lax.broadcasted_iota(jnp.int32, (lc, lc), 0)
    jj = lax.broadcasted_iota(jnp.int32, (lc, lc), 1)
    tril = jj <= ii
    tri = jnp.where(tril, 1.0, 0.0).astype(BF16)
    a3 = jnp.concatenate(_split3(a), axis=1)
    c3 = jnp.dot(tri, a3, preferred_element_type=F32)
    acum = c3[:, :LANES] + c3[:, LANES:2 * LANES] + c3[:, 2 * LANES:]

    er = lax.broadcasted_iota(jnp.int32, (LANES, inner), 0)
    ec = lax.broadcasted_iota(jnp.int32, (LANES, inner), 1)
    expand = jnp.where(ec // SSD_HEAD_DIM == er, 1.0, 0.0).astype(BF16)
    dt_hi = dt.astype(BF16)
    dt_lo = (dt - dt_hi.astype(F32)).astype(BF16)
    stack = jnp.concatenate(list(_split3(acum)) + [dt_hi, dt_lo], axis=0)
    ex = jnp.dot(stack, expand, preferred_element_type=F32)
    acum_e = ex[:lc] + ex[lc:2 * lc] + ex[2 * lc:3 * lc]
    dt_e = ex[3 * lc:4 * lc] + ex[4 * lc:]

    acum_last = acum_e[lc - 1:lc, :]
    xd = xs * dt_e
    xd_b = xd.astype(BF16)
    xd_end = (xd * jnp.exp(acum_last - acum_e)).astype(BF16)
    dec_in = jnp.exp(acum_e)
    dec_all = jnp.exp(acum_last)
    acum_t = acum.T

    lane = lax.broadcasted_iota(jnp.int32, (lc, LANES), 1)
    y_parts = []
    for g in range(SSD_GROUPS):
        bg = xc[:, bc0 + g * SSD_STATE:bc0 + (g + 1) * SSD_STATE]
        cg = xc[:, cc0 + g * SSD_STATE:cc0 + (g + 1) * SSD_STATE].astype(BF16)
        gsl = slice(g * gw, (g + 1) * gw)
        cbt = lax.dot_general(cg, bg.astype(BF16), (((1,), (1,)), ((), ())),
                              preferred_element_type=F32)
        h_prev = h_scr[:, gsl]
        y_off = jnp.dot(cg, h_prev.astype(BF16), preferred_element_type=F32) * dec_in[:, gsl]
        st = jnp.dot(bg.T.astype(BF16), xd_end[:, gsl], preferred_element_type=F32)
        h_scr[:, gsl] = h_prev * dec_all[:, gsl] + st
        heads_per_group = SSD_HEADS // SSD_GROUPS
        for pr in range(heads_per_group // 2):
            r0 = g * heads_per_group + 2 * pr
            ms = []
            for r in (r0, r0 + 1):
                seg = acum[:, r:r + 1] - acum_t[r:r + 1, :]
                ms.append((cbt * jnp.exp(jnp.where(tril, seg, NEG_BIG))).astype(BF16))
            xp = xd_b[:, r0 * SSD_HEAD_DIM:r0 * SSD_HEAD_DIM + LANES]
            zero = jnp.zeros_like(xp)
            xblk = jnp.concatenate([jnp.where(lane < SSD_HEAD_DIM, xp, zero),
                                    jnp.where(lane >= SSD_HEAD_DIM, xp, zero)], axis=0)
            y_diag = jnp.dot(jnp.concatenate(ms, axis=1), xblk, preferred_element_type=F32)
            lsl = slice(pr * LANES, (pr + 1) * LANES)
            y_parts.append(y_diag + y_off[:, lsl])
    y = jnp.concatenate(y_parts, axis=1) + xs * dexp_ref[...]

    gated = y * _silu(z_ref[...].astype(F32))
    outs = []
    for g in range(SSD_GROUPS):
        gg = gated[:, g * gw:(g + 1) * gw]
        outs.append(gg * lax.rsqrt(jnp.mean(gg * gg, axis=-1, keepdims=True) + EPS))
    y_ref[...] = (jnp.concatenate(outs, axis=1) * nw_ref[...]).astype(BF16)


def _ssd(z, xbc, dt, cw, cb, dtb, alog, dexp, nw, lc):
    bsz, seq, inner = z.shape
    cdim = xbc.shape[-1]
    full = lambda c: (0, 0)
    return pl.pallas_call(
        functools.partial(_ssd_kernel, lc=lc),
        grid=(seq // lc,),
        in_specs=[pl.BlockSpec((bsz, lc, inner), lambda c: (0, c, 0)),
                  pl.BlockSpec((bsz, lc, cdim), lambda c: (0, c, 0)),
                  pl.BlockSpec((bsz, lc, LANES), lambda c: (0, c, 0)),
                  pl.BlockSpec((SSD_CONV, cdim), full),
                  pl.BlockSpec((1, cdim), full),
                  pl.BlockSpec((1, LANES), full),
                  pl.BlockSpec((1, LANES), full),
                  pl.BlockSpec((1, inner), full),
                  pl.BlockSpec((1, inner), full)],
        out_specs=pl.BlockSpec((bsz, lc, inner), lambda c: (0, c, 0)),
        out_shape=jax.ShapeDtypeStruct((bsz, seq, inner), BF16),
        scratch_shapes=[pltpu.VMEM((bsz, SSD_STATE, inner), F32),
                        pltpu.VMEM((bsz, HALO + lc, cdim), BF16)],
        compiler_params=_cparams("arbitrary"),
        name="ssd",
    )(z, xbc, dt, cw, cb, dtb, alog, dexp, nw)


def _merge_kernel(x_ref, ya_ref, ys_ref, gp_ref, gb_ref, wa_ref, ws_ref, wo_ref, o_ref):
    d = x_ref.shape[-1]
    gates = _sigmoid(gp_ref[...].astype(F32) + gb_ref[...])
    pa = lax.dot_general(ya_ref[0], wa_ref[...], (((0,), (0,)), ((), ())),
                         preferred_element_type=F32)
    ps = jnp.dot(ys_ref[...], ws_ref[...], preferred_element_type=F32)
    merged = gates[:, :d] * pa + gates[:, d:] * ps
    o_ref[...] = x_ref[...] + jnp.dot(merged.astype(BF16), wo_ref[...], preferred_element_type=F32)


def _merge(x, ya, ys, gp, gb, wa, ws, wo, tm):
    t, d = x.shape
    tiles_per_seq = ya.shape[2] // tm
    row = lambda i: (i, 0)
    full = lambda i: (0, 0)
    return pl.pallas_call(
        _merge_kernel,
        grid=(t // tm,),
        in_specs=[pl.BlockSpec((tm, d), row),
                  pl.BlockSpec((1, ya.shape[1], tm),
                               lambda i: (i // tiles_per_seq, 0, i % tiles_per_seq)),
                  pl.BlockSpec((tm, ys.shape[1]), row),
                  pl.BlockSpec((tm, gp.shape[1]), row),
                  pl.BlockSpec(gb.shape, full),
                  pl.BlockSpec(wa.shape, full),
                  pl.BlockSpec(ws.shape, full),
                  pl.BlockSpec(wo.shape, full)],
        out_specs=pl.BlockSpec((tm, d), row),
        out_shape=jax.ShapeDtypeStruct((t, d), F32),
        compiler_params=_cparams("parallel"),
        name="merge",
    )(x, ya, ys, gp, gb, wa, ws, wo)


def _ffn_kernel(res_ref, halo_ref, nw_ref, wup_ref, cw_ref, cb_ref, wd_ref, fnw_ref, o_ref, h_scr,
                u_scr, *, tm, hid, tiles_per_seq, final_norm):
    i = pl.program_id(0)
    nw = nw_ref[...]
    hh = _rmsnorm(halo_ref[...], nw)
    hh = jnp.where(i % tiles_per_seq == 0, 0.0, hh)
    h_scr[0:HALO, :] = hh.astype(BF16)
    h_scr[HALO:, :] = _rmsnorm(res_ref[...], nw).astype(BF16)
    th = hid // FFN_SPLIT
    rows = tm // FFN_ROW_PARTS

    for part in range(FFN_ROW_PARTS):
        r0 = part * rows
        hx = h_scr[r0:r0 + HALO + rows, :]

        def branch(c0, slot):
            u_scr[slot] = jnp.dot(hx, wup_ref[:, c0:c0 + th], preferred_element_type=F32)
            cw = cw_ref[:, c0:c0 + th]
            out = cb_ref[:, c0:c0 + th]
            for kk in range(FFN_CONV):
                off = HALO - (FFN_CONV - 1) + kk
                out = out + cw[kk:kk + 1] * u_scr[slot, off:off + rows, :]
            return out

        total = res_ref[r0:r0 + rows, :]
        for c in range(FFN_SPLIT):
            slot = 2 * (part * FFN_SPLIT + c)
            act = (_silu(branch(hid + c * th, slot)) * branch(c * th, slot + 1)).astype(BF16)
            total = total + jnp.dot(act, wd_ref[c * th:(c + 1) * th, :],
                                    preferred_element_type=F32)
        o_ref[r0:r0 + rows, :] = _rmsnorm(total, fnw_ref[...]) if final_norm else total


def _ffn(res, nw, w_up, cw, cb, w_down, fnw, seq, tm, final_norm):
    t, d = res.shape
    hid = w_down.shape[0]
    row = lambda i: (i, 0)
    full = lambda i: (0, 0)
    once = pl.Buffered(1)
    halo_blocks = tm // HALO
    return pl.pallas_call(
        functools.partial(_ffn_kernel, tm=tm, hid=hid, tiles_per_seq=seq // tm,
                          final_norm=final_norm),
        grid=(t // tm,),
        in_specs=[pl.BlockSpec((tm, d), row),
                  pl.BlockSpec((HALO, d), lambda i: (jnp.maximum(i * halo_blocks - 1, 0), 0)),
                  pl.BlockSpec((1, d), full),
                  pl.BlockSpec(w_up.shape, full, pipeline_mode=once),
                  pl.BlockSpec(cw.shape, full, pipeline_mode=once),
                  pl.BlockSpec(cb.shape, full, pipeline_mode=once),
                  pl.BlockSpec(w_down.shape, full, pipeline_mode=once),
                  pl.BlockSpec((1, d), full)],
        out_specs=pl.BlockSpec((tm, d), row),
        out_shape=jax.ShapeDtypeStruct((t, d), F32),
        scratch_shapes=[pltpu.VMEM((HALO + tm, d), BF16),
                        pltpu.VMEM((2 * FFN_SPLIT * FFN_ROW_PARTS, HALO + tm // FFN_ROW_PARTS,
                                    hid // FFN_SPLIT), F32)],
        compiler_params=_cparams("parallel"),
        name="ffn",
    )(res, res, nw, w_up, cw, cb, w_down, fnw)


def kernel(x, mix_norm_w, w_in, gate_b, lambda_q1, lambda_k1, lambda_q2, lambda_k2, attn_subln_w,
           ssd_conv_w, ssd_conv_b, ssd_dt_bias, ssd_a_log, ssd_d, ssd_norm_w, w_attn_branch,
           w_ssd_branch, w_out, ffn_norm_w, w_up, ffn_conv_w, ffn_conv_b, w_down, final_norm_w):
    bsz, seq, d = x.shape
    t = bsz * seq
    depth = w_in.shape[0]
    att_w = ATT_HEADS * ATT_V_DIM
    inner = SSD_HEADS * SSD_HEAD_DIM
    cdim = inner + 2 * SSD_GROUPS * SSD_STATE
    c_q, c_k, c_v, c_z, c_x, c_dt = (att_w, 2 * att_w, 3 * att_w, 3 * att_w + inner,
                                     3 * att_w + inner + cdim, 3 * att_w + inner + cdim + SSD_HEADS)
    tk = min(ATT_KEYS, seq // 2)
    lc = min(SSD_TILE, seq)
    tm = min(ROW_TILE, seq)
    mm_rows = min(MM_ROWS, t)
    slopes = (jnp.exp2(-8.0 * jnp.arange(1, ATT_HEADS + 1, dtype=F32) / ATT_HEADS) * LOG2E).astype(F32)
    pad_heads = lambda v: jnp.pad(v.astype(F32), (0, LANES - SSD_HEADS)).reshape(1, LANES)

    res = x.reshape(t, d)
    for l in range(depth):
        lam_init = 0.8 - 0.6 * math.exp(-0.3 * l)
        wt = jnp.swapaxes(w_in[l], 0, 1).astype(BF16)
        wt_gate = wt[c_dt:]
        wdt = jnp.pad(wt[c_x:c_dt], ((0, LANES - SSD_HEADS), (0, 0)))

        h, dt_raw, k = _norm_dt(res, mix_norm_w[l].reshape(1, d), wdt, wt, c_q // att_w, att_w,
                                mm_rows)
        blocks = lambda c0, c1: list(range(c0 // att_w, c1 // att_w))
        z = _matmul_nt(h, wt, blocks(c_v, c_z), att_w, mm_rows, "proj_z")
        xbc = _matmul_nt(h, wt, blocks(c_z, c_x), att_w, mm_rows, "proj_xbc")
        gate_pre = _matmul_nt(h, wt_gate, blocks(0, wt_gate.shape[0]), att_w, mm_rows,
                              "proj_gate")
        qvt = _proj_t(wt, 0, c_k // att_w, att_w, ATT_HEAD_DIM ** -0.5 * LOG2E, h, bsz, seq, tk,
                      min(MM_ROWS, seq))

        lamv = jnp.stack([lambda_q1[l], lambda_k1[l], lambda_q2[l], lambda_k2[l]]).astype(F32)
        y_att = _attention(slopes, lamv, attn_subln_w[l].reshape(ATT_V_DIM, 1).astype(F32), qvt,
                           k.reshape(bsz, seq, att_w), bsz, seq, tk, lam_init)
        y_ssd = _ssd(z.reshape(bsz, seq, inner), xbc.reshape(bsz, seq, cdim),
                     dt_raw.reshape(bsz, seq, LANES), ssd_conv_w[l].astype(F32),
                     ssd_conv_b[l].reshape(1, cdim).astype(F32), pad_heads(ssd_dt_bias[l]),
                     pad_heads(ssd_a_log[l]),
                     jnp.repeat(ssd_d[l].astype(F32), SSD_HEAD_DIM).reshape(1, inner),
                     ssd_norm_w[l].reshape(1, inner).astype(F32), lc)
        res = _merge(res, y_att, y_ssd.reshape(t, inner), gate_pre,
                     gate_b[l].reshape(1, -1).astype(F32), w_attn_branch[l].astype(BF16),
                     w_ssd_branch[l].astype(BF16), w_out[l].astype(BF16), tm)
        res = _ffn(res, ffn_norm_w[l].reshape(1, d).astype(F32), w_up[l].astype(BF16),
                   ffn_conv_w[l].astype(F32), ffn_conv_b[l].reshape(1, -1).astype(F32),
                   w_down[l].astype(BF16), final_norm_w.reshape(1, d).astype(F32), seq, tm,
                   l == depth - 1)
    return res.reshape(bsz, seq, d)
```

```python
import functools
import math

import jax
import jax.numpy as jnp
from jax import lax
from jax.experimental import pallas as pl
from jax.experimental.pallas import tpu as pltpu

F32 = jnp.float32
BF16 = jnp.bfloat16

EPS = 1e-6
LOG2E = math.log2(math.e)
NEG_BIG = -1e30

ATT_HEADS = 8
ATT_HEAD_DIM = 64
ATT_V_DIM = 2 * ATT_HEAD_DIM
ATT_CHUNK = 64
SSD_HEADS = 32
SSD_HEAD_DIM = 64
SSD_GROUPS = 4
SSD_STATE = 128
SSD_CONV = 4
FFN_CONV = 3

LANES = 128
VMEM_LIMIT_BYTES = 56 * 1024 * 1024

ATT_KEYS = 256
ATT_HEADS_PER_STEP = 2
SSD_TILE = 128
ROW_TILE = 512
MM_ROWS = 1024
FFN_SPLIT = 1
FFN_ROW_PARTS = 1
HALO = 16


def _cparams(*sem):
    return pltpu.CompilerParams(dimension_semantics=sem, vmem_limit_bytes=VMEM_LIMIT_BYTES)


def _sigmoid(x):
    return 0.5 * (jnp.tanh(0.5 * x) + 1.0)


def _silu(x):
    h = 0.5 * x
    return h * (jnp.tanh(h) + 1.0)


def _rmsnorm(x, w):
    ms = jnp.mean(x * x, axis=-1, keepdims=True)
    return x * lax.rsqrt(ms + EPS) * w


def _split3(x):
    hi = x.astype(BF16)
    r1 = x - hi.astype(F32)
    mid = r1.astype(BF16)
    lo = (r1 - mid.astype(F32)).astype(BF16)
    return hi, mid, lo


_NT = (((1,), (1,)), ((), ()))


def _norm_dt_kernel(x_ref, w_ref, wdt_ref, wk_ref, h_ref, dt_ref, k_ref):
    h = _rmsnorm(x_ref[...], w_ref[...]).astype(BF16)
    h_ref[...] = h
    dt_ref[...] = lax.dot_general(h, wdt_ref[...], _NT, preferred_element_type=F32)
    k_ref[...] = lax.dot_general(h, wk_ref[...], _NT, preferred_element_type=F32).astype(BF16)


def _norm_dt(x, w, wdt, wt, k_block, blk, tm):
    t, d = x.shape
    return pl.pallas_call(
        _norm_dt_kernel,
        grid=(t // tm,),
        in_specs=[pl.BlockSpec((tm, d), lambda i: (i, 0)),
                  pl.BlockSpec((1, d), lambda i: (0, 0)),
                  pl.BlockSpec((LANES, d), lambda i: (0, 0)),
                  pl.BlockSpec((blk, d), lambda i: (k_block, 0))],
        out_specs=[pl.BlockSpec((tm, d), lambda i: (i, 0)),
                   pl.BlockSpec((tm, LANES), lambda i: (i, 0)),
                   pl.BlockSpec((tm, blk), lambda i: (i, 0))],
        out_shape=[jax.ShapeDtypeStruct((t, d), BF16),
                   jax.ShapeDtypeStruct((t, LANES), F32),
                   jax.ShapeDtypeStruct((t, blk), BF16)],
        compiler_params=_cparams("parallel"),
        name="norm_dt",
    )(x, w, wdt, wt)


def _mm_nt_kernel(a_ref, *refs, counts):
    w_refs, o_refs = refs[:sum(counts)], refs[sum(counts):]
    a = a_ref[...]
    w_iter = iter(w_refs)
    for o_ref, count in zip(o_refs, counts):
        for j in range(count):
            w_ref = next(w_iter)
            n = w_ref.shape[0]
            o_ref[:, j * n:(j + 1) * n] = lax.dot_general(
                a, w_ref[...], _NT, preferred_element_type=F32).astype(o_ref.dtype)


def _matmul_nt(a, outs, blk, tm, name):
    m, k = a.shape
    counts = tuple(len(rb) for _, rb in outs)
    once = pl.Buffered(1)
    return pl.pallas_call(
        functools.partial(_mm_nt_kernel, counts=counts),
        grid=(m // tm,),
        in_specs=[pl.BlockSpec((tm, k), lambda i: (i, 0))]
        + [pl.BlockSpec((blk, k), lambda i, r=r: (r, 0), pipeline_mode=once)
           for _, rb in outs for r in rb],
        out_specs=[pl.BlockSpec((tm, blk * c), lambda i: (i, 0)) for c in counts],
        out_shape=[jax.ShapeDtypeStruct((m, blk * c), BF16) for c in counts],
        compiler_params=_cparams("parallel"),
        name=name,
    )(a, *[wt for wt, rb in outs for _ in rb])


def _proj_t_kernel(wq_ref, wv_ref, h_ref, o_ref, *, tq, q_scale):
    nq = wq_ref.shape[0]
    for s in range(o_ref.shape[1]):
        hs = h_ref[s * tq:(s + 1) * tq, :]
        qt = lax.dot_general(wq_ref[...], hs, _NT, preferred_element_type=F32)
        o_ref[0, s, 0:nq, :] = (qt * q_scale).astype(BF16)
        o_ref[0, s, nq:, :] = lax.dot_general(wv_ref[...], hs, _NT,
                                              preferred_element_type=F32).astype(BF16)


def _proj_t(wt, q_block, v_block, blk, q_scale, h, bsz, seq, tq, tm):
    d = wt.shape[1]
    nq = seq // tq
    per_step = tm // tq
    steps = seq // tm
    return pl.pallas_call(
        functools.partial(_proj_t_kernel, tq=tq, q_scale=q_scale),
        grid=(bsz, steps),
        in_specs=[pl.BlockSpec((blk, d), lambda b, i: (q_block, 0)),
                  pl.BlockSpec((blk, d), lambda b, i: (v_block, 0)),
                  pl.BlockSpec((tm, d), lambda b, i: (b * steps + i, 0))],
        out_specs=pl.BlockSpec((1, per_step, 2 * blk, tq), lambda b, i: (b, i, 0, 0)),
        out_shape=jax.ShapeDtypeStruct((bsz, nq, 2 * blk, tq), BF16),
        compiler_params=_cparams("parallel", "parallel"),
        name="proj_t",
    )(wt, wt, h)


def _attn_kernel(slopes_ref, lamv_ref, subw_ref, qt_ref, k_ref, vt_ref, o_ref, rhs_ref, kaug_ref,
                 s_ref, p_ref, acc_ref, stat_ref, dx_ref, *, tk, lam_init):
    nh = ATT_HEADS_PER_STEP
    tq = 2 * tk
    hp = pl.program_id(1)
    qi = pl.program_id(2)
    slope = [slopes_ref[hp * nh + hh] for hh in range(nh)]

    mx_i = 0
    head_0 = 4 * nh
    n_stat = 6
    al_i, m_i, l_i = 0, 2, 4

    row = lax.broadcasted_iota(jnp.int32, (ATT_V_DIM, tq), 0)
    ridx = lax.broadcasted_iota(jnp.int32, (8, tq), 0)
    rpos = lax.broadcasted_iota(jnp.int32, (8, tq), 1).astype(F32)
    for hh in range(nh):
        a = slope[hh]
        hsl = slice(hh * ATT_V_DIM, (hh + 1) * ATT_V_DIM)
        qt = jnp.concatenate([qt_ref[0, 0, hsl, :], qt_ref[0, 1, hsl, :]], axis=1)
        zero = jnp.zeros_like(qt)
        val = jnp.where(ridx < 3, a, -a * rpos)
        hi = val.astype(BF16).astype(F32)
        r1 = val - hi
        mid = r1.astype(BF16).astype(F32)
        lo = r1 - mid
        sel = ridx % 3
        piece = jnp.where(sel == 0, hi, jnp.where(sel == 1, mid, lo))
        piece = jnp.where(ridx < 6, piece, 0.0)
        aug = jnp.concatenate([piece, jnp.zeros((LANES - 8, tq), F32)], axis=0).astype(BF16)
        rhs_ref[2 * hh] = jnp.concatenate([jnp.where(row < ATT_HEAD_DIM, qt, zero), aug], axis=0)
        rhs_ref[2 * hh + 1] = jnp.concatenate([jnp.where(row >= ATT_HEAD_DIM, qt, zero), aug],
                                              axis=0)
        for mi in range(2):
            base = head_0 + hh * n_stat
            stat_ref[base + m_i + mi] = jnp.full((8, tq), NEG_BIG, F32)
            stat_ref[base + l_i + mi] = jnp.zeros((8, tq), F32)

    cpos = lax.broadcasted_iota(jnp.int32, (tk, LANES), 0).astype(F32)
    lane = lax.broadcasted_iota(jnp.int32, (tk, LANES), 1)
    kaug_ref[...] = jnp.where(lane < 3, cpos, jnp.where(lane < 6, 1.0, 0.0)).astype(BF16)
    acc_ref[...] = jnp.zeros_like(acc_ref)

    groups = [(hh, mi) for hh in range(nh) for mi in range(2)]

    def scores(kt, hh, mi, diag):
        hsl = slice(hh * ATT_V_DIM, (hh + 1) * ATT_V_DIM)
        for u in range(2):
            start = pl.multiple_of((2 * kt + u) * tk, tk)
            kf = jnp.concatenate([k_ref[0, pl.ds(start, tk), hsl], kaug_ref[...]], axis=1)
            s = jnp.dot(kf, rhs_ref[2 * hh + mi], preferred_element_type=F32)
            if diag:
                if u == 0:
                    s = jnp.concatenate([s[:, :tk] + dx_ref[hh], s[:, tk:]], axis=1)
                else:
                    s = jnp.concatenate([jnp.full((tk, tk), NEG_BIG, F32), s[:, tk:] + dx_ref[hh]],
                                        axis=1)
            sb = 4 * hh + 2 * mi + u
            s_ref[sb] = s
            stat_ref[mx_i + sb, 0:1, :] = jnp.max(s, axis=0, keepdims=True)

    def softmax(kt, hh, mi):
        base = head_0 + hh * n_stat
        cj = [-slope[hh] * (qi * tq - (2 * kt + u) * tk).astype(F32) for u in range(2)]
        sb = 4 * hh + 2 * mi
        m_old = stat_ref[base + m_i + mi, 0:1, :]
        mx = jnp.maximum(stat_ref[mx_i + sb, 0:1, :] + cj[0],
                         stat_ref[mx_i + sb + 1, 0:1, :] + cj[1])
        m_new = jnp.maximum(m_old, mx)
        alpha = jnp.exp2(m_old - m_new)
        lsum = alpha * stat_ref[base + l_i + mi, 0:1, :]
        for u in range(2):
            p = jnp.exp2(s_ref[sb + u] - (m_new - cj[u]))
            lsum = lsum + jnp.sum(p, axis=0, keepdims=True)
            p_ref[2 * hh + mi, u * tk:(u + 1) * tk, :] = p.astype(BF16)
        stat_ref[base + l_i + mi, 0:1, :] = lsum
        stat_ref[base + m_i + mi, 0:1, :] = m_new
        stat_ref[base + al_i + mi, 0:1, :] = alpha

    def values(kt, hh, mi):
        hsl = slice(hh * ATT_V_DIM, (hh + 1) * ATT_V_DIM)
        vt = jnp.concatenate([vt_ref[0, 2 * kt, hsl, :], vt_ref[0, 2 * kt + 1, hsl, :]], axis=1)
        acc_ref[2 * hh + mi] = (
            stat_ref[head_0 + hh * n_stat + al_i + mi, 0:1, :] * acc_ref[2 * hh + mi]
            + jnp.dot(vt, p_ref[2 * hh + mi], preferred_element_type=F32))

    ci = lax.broadcasted_iota(jnp.int32, (tk, tk), 0)
    ri = lax.broadcasted_iota(jnp.int32, (tk, tk), 1)
    for hh in range(nh):
        fix = jnp.where(ci > ri, (2.0 * slope[hh]) * (ri - ci).astype(F32), 0.0)
        dx_ref[hh] = jnp.where((ci // ATT_CHUNK) <= (ri // ATT_CHUNK), fix, NEG_BIG)
    for hh, mi in groups:
        scores(qi, hh, mi, True)

    for hh, mi in groups:
        softmax(qi, hh, mi)
    for hh, mi in groups:
        scores(0, hh, mi, False)

    def time_step(s):
        kv = jnp.where(s == 2, qi, s - 3)
        for hh, mi in groups:
            values(kv, hh, mi)
        for hh, mi in groups:
            softmax(s - 2, hh, mi)
        for hh, mi in groups:
            scores(s - 1, hh, mi, False)

    def body2(i, carry):
        time_step(2 * i + 2)
        time_step(2 * i + 3)
        return carry

    def body1(s, carry):
        time_step(s)
        return carry

    pairs = jnp.maximum(qi - 1, 0) // 2
    lax.fori_loop(0, pairs, body2, 0)
    lax.fori_loop(jnp.minimum(2 * pairs + 2, qi + 1), qi + 1, body1, 0)

    older = jnp.where(qi <= 1, qi, qi - 2)
    for hh, mi in groups:
        values(older, hh, mi)

    @pl.when(qi >= 1)
    def _():
        for hh, mi in groups:
            softmax(qi - 1, hh, mi)
        for hh, mi in groups:
            values(qi - 1, hh, mi)

    lv = lamv_ref[...]
    t1 = jnp.sum(lv[0:1] * lv[1:2], axis=1, keepdims=True)
    t2 = jnp.sum(lv[2:3] * lv[3:4], axis=1, keepdims=True)
    lam = jnp.exp(t1) - jnp.exp(t2) + lam_init
    subw = subw_ref[...] * (1.0 - lam_init)
    for hh in range(nh):
        base = head_0 + hh * n_stat
        o = (acc_ref[2 * hh] * (1.0 / stat_ref[base + l_i, 0:1, :])
             - lam * (acc_ref[2 * hh + 1] * (1.0 / stat_ref[base + l_i + 1, 0:1, :])))
        o = o * lax.rsqrt(jnp.mean(o * o, axis=0, keepdims=True) + EPS)
        o_ref[0, hh * ATT_V_DIM:(hh + 1) * ATT_V_DIM, :] = (o * subw).astype(BF16)


def _attention(slopes, lamv, subw, qvt, k, bsz, seq, tk, lam_init):
    nh = ATT_HEADS_PER_STEP
    tq = 2 * tk
    nq = seq // tq
    nk = seq // tk
    d = ATT_HEADS * ATT_V_DIM
    hw = nh * ATT_V_DIM
    return pl.pallas_call(
        functools.partial(_attn_kernel, tk=tk, lam_init=lam_init),
        grid=(bsz, ATT_HEADS // nh, nq),
        in_specs=[pl.BlockSpec(memory_space=pltpu.SMEM),
                  pl.BlockSpec((4, ATT_HEAD_DIM), lambda b, h, i: (0, 0)),
                  pl.BlockSpec((ATT_V_DIM, 1), lambda b, h, i: (0, 0)),
                  pl.BlockSpec((1, 2, hw, tk), lambda b, h, i: (b, i, h, 0)),
                  pl.BlockSpec((1, seq, hw), lambda b, h, i: (b, 0, h)),
                  pl.BlockSpec((1, nk, hw, tk), lambda b, h, i: (b, 0, ATT_HEADS // nh + h, 0))],
        out_specs=pl.BlockSpec((1, hw, tq), lambda b, h, i: (b, h, i)),
        out_shape=jax.ShapeDtypeStruct((bsz, d, seq), BF16),
        scratch_shapes=[pltpu.VMEM((2 * nh, 2 * LANES, tq), BF16),
                        pltpu.VMEM((tk, LANES), BF16),
                        pltpu.VMEM((4 * nh, tk, tq), F32),
                        pltpu.VMEM((2 * nh, tq, tq), BF16),
                        pltpu.VMEM((2 * nh, ATT_V_DIM, tq), F32),
                        pltpu.VMEM((10 * nh, 8, tq), F32),
                        pltpu.VMEM((nh, tk, tk), F32)],
        compiler_params=_cparams("parallel", "parallel", "arbitrary"),
        name="attn",
    )(slopes, lamv, subw, qvt, k, qvt)


def _ssd_kernel(z_ref, xbc_ref, dt_ref, cw_ref, cb_ref, dtb_ref, alog_ref, dexp_ref, nw_ref,
                y_ref, h_scr, xpad_scr, *, lc):
    first = pl.program_id(0) == 0
    for b in range(z_ref.shape[0]):
        _ssd_sequence_step(first, z_ref.at[b], xbc_ref.at[b], dt_ref.at[b], cw_ref, cb_ref,
                           dtb_ref, alog_ref, dexp_ref, nw_ref, y_ref.at[b], h_scr.at[b],
                           xpad_scr.at[b], lc=lc)


def _ssd_sequence_step(first, z_ref, xbc_ref, dt_ref, cw_ref, cb_ref, dtb_ref, alog_ref, dexp_ref,
                       nw_ref, y_ref, h_scr, xpad_scr, *, lc):
    inner = SSD_HEADS * SSD_HEAD_DIM
    gw = inner // SSD_GROUPS
    bc0 = inner
    cc0 = inner + SSD_GROUPS * SSD_STATE

    @pl.when(first)
    def _():
        h_scr[...] = jnp.zeros_like(h_scr)
        xpad_scr[0:HALO, :] = jnp.zeros((HALO, xpad_scr.shape[1]), BF16)

    xraw = xbc_ref[...]
    xpad_scr[HALO:, :] = xraw
    xpad = xpad_scr[...]
    cw = cw_ref[...]
    conv = cb_ref[...] + cw[SSD_CONV - 1:SSD_CONV] * xraw.astype(F32)
    srow = lax.broadcasted_iota(jnp.int32, (lc, HALO + lc), 0)
    scol = lax.broadcasted_iota(jnp.int32, (lc, HALO + lc), 1)
    for kk in range(SSD_CONV - 1):
        shift = jnp.where(scol == srow + (HALO - (SSD_CONV - 1) + kk), 1.0, 0.0).astype(BF16)
        conv = conv + cw[kk:kk + 1] * jnp.dot(shift, xpad, preferred_element_type=F32)
    xpad_scr[0:HALO, :] = xraw[lc - HALO:, :]
    xc = _silu(conv)
    xs = xc[:, :inner]

    x = dt_ref[...] + dtb_ref[...]
    dt = jnp.maximum(x, 0.0) + jnp.log(1.0 + jnp.exp(-jnp.abs(x)))
    a = dt * (-jnp.exp(alog_ref[...]))

    ii = lax.broadcasted_iota(jnp.int32, (lc, lc), 0)
    jj = lax.broadcasted_iota(jnp.int32, (lc, lc), 1)
    tril = jj <= ii
    tri = jnp.where(tril, 1.0, 0.0).astype(BF16)
    a3 = jnp.concatenate(_split3(a), axis=1)
    c3 = jnp.dot(tri, a3, preferred_element_type=F32)
    acum = c3[:, :LANES] + c3[:, LANES:2 * LANES] + c3[:, 2 * LANES:]

    er = lax.broadcasted_iota(jnp.int32, (LANES, inner), 0)
    ec = lax.broadcasted_iota(jnp.int32, (LANES, inner), 1)
    expand = jnp.where(ec // SSD_HEAD_DIM == er, 1.0, 0.0).astype(BF16)
    dt_hi = dt.astype(BF16)
    dt_lo = (dt - dt_hi.astype(F32)).astype(BF16)
    stack = jnp.concatenate(list(_split3(acum)) + [dt_hi, dt_lo], axis=0)
    ex = jnp.dot(stack, expand, preferred_element_type=F32)
    acum_e = ex[:lc] + ex[lc:2 * lc] + ex[2 * lc:3 * lc]
    dt_e = ex[3 * lc:4 * lc] + ex[4 * lc:]

    acum_last = acum_e[lc - 1:lc, :]
    xd = xs * dt_e
    xd_b = xd.astype(BF16)
    xd_end = (xd * jnp.exp(acum_last - acum_e)).astype(BF16)
    dec_in = jnp.exp(acum_e)
    dec_all = jnp.exp(acum_last)
    acum_t = acum.T

    lane = lax.broadcasted_iota(jnp.int32, (lc, LANES), 1)
    y_parts = []
    for g in range(SSD_GROUPS):
        bg = xc[:, bc0 + g * SSD_STATE:bc0 + (g + 1) * SSD_STATE]
        cg = xc[:, cc0 + g * SSD_STATE:cc0 + (g + 1) * SSD_STATE].astype(BF16)
        gsl = slice(g * gw, (g + 1) * gw)
        cbt = lax.dot_general(cg, bg.astype(BF16), (((1,), (1,)), ((), ())),
                              preferred_element_type=F32)
        h_prev = h_scr[:, gsl]
        y_off = jnp.dot(cg, h_prev.astype(BF16), preferred_element_type=F32) * dec_in[:, gsl]
        st = jnp.dot(bg.T.astype(BF16), xd_end[:, gsl], preferred_element_type=F32)
        h_scr[:, gsl] = h_prev * dec_all[:, gsl] + st
        heads_per_group = SSD_HEADS // SSD_GROUPS
        for pr in range(heads_per_group // 2):
            r0 = g * heads_per_group + 2 * pr
            ms = []
            for r in (r0, r0 + 1):
                seg = acum[:, r:r + 1] - acum_t[r:r + 1, :]
                ms.append((cbt * jnp.exp(jnp.where(tril, seg, NEG_BIG))).astype(BF16))
            xp = xd_b[:, r0 * SSD_HEAD_DIM:r0 * SSD_HEAD_DIM + LANES]
            zero = jnp.zeros_like(xp)
            xblk = jnp.concatenate([jnp.where(lane < SSD_HEAD_DIM, xp, zero),
                                    jnp.where(lane >= SSD_HEAD_DIM, xp, zero)], axis=0)
            y_diag = jnp.dot(jnp.concatenate(ms, axis=1), xblk, preferred_element_type=F32)
            lsl = slice(pr * LANES, (pr + 1) * LANES)
            y_parts.append(y_diag + y_off[:, lsl])
    y = jnp.concatenate(y_parts, axis=1) + xs * dexp_ref[...]

    gated = y * _silu(z_ref[...].astype(F32))
    outs = []
    for g in range(SSD_GROUPS):
        gg = gated[:, g * gw:(g + 1) * gw]
        outs.append(gg * lax.rsqrt(jnp.mean(gg * gg, axis=-1, keepdims=True) + EPS))
    y_ref[...] = (jnp.concatenate(outs, axis=1) * nw_ref[...]).astype(BF16)


def _ssd(z, xbc, dt, cw, cb, dtb, alog, dexp, nw, lc):
    bsz, seq, inner = z.shape
    cdim = xbc.shape[-1]
    full = lambda c: (0, 0)
    return pl.pallas_call(
        functools.partial(_ssd_kernel, lc=lc),
        grid=(seq // lc,),
        in_specs=[pl.BlockSpec((bsz, lc, inner), lambda c: (0, c, 0)),
                  pl.BlockSpec((bsz, lc, cdim), lambda c: (0, c, 0)),
                  pl.BlockSpec((bsz, lc, LANES), lambda c: (0, c, 0)),
                  pl.BlockSpec((SSD_CONV, cdim), full),
                  pl.BlockSpec((1, cdim), full),
                  pl.BlockSpec((1, LANES), full),
                  pl.BlockSpec((1, LANES), full),
                  pl.BlockSpec((1, inner), full),
                  pl.BlockSpec((1, inner), full)],
        out_specs=pl.BlockSpec((bsz, lc, inner), lambda c: (0, c, 0)),
        out_shape=jax.ShapeDtypeStruct((bsz, seq, inner), BF16),
        scratch_shapes=[pltpu.VMEM((bsz, SSD_STATE, inner), F32),
                        pltpu.VMEM((bsz, HALO + lc, cdim), BF16)],
        compiler_params=_cparams("arbitrary"),
        name="ssd",
    )(z, xbc, dt, cw, cb, dtb, alog, dexp, nw)


def _merge_kernel(x_ref, ya_ref, ys_ref, gp_ref, gb_ref, wa_ref, ws_ref, wo_ref, o_ref):
    d = x_ref.shape[-1]
    gates = _sigmoid(gp_ref[...].astype(F32) + gb_ref[...])
    pa = lax.dot_general(ya_ref[0], wa_ref[...], (((0,), (0,)), ((), ())),
                         preferred_element_type=F32)
    ps = jnp.dot(ys_ref[...], ws_ref[...], preferred_element_type=F32)
    merged = gates[:, :d] * pa + gates[:, d:] * ps
    o_ref[...] = x_ref[...] + jnp.dot(merged.astype(BF16), wo_ref[...], preferred_element_type=F32)


def _merge(x, ya, ys, gp, gb, wa, ws, wo, tm):
    t, d = x.shape
    tiles_per_seq = ya.shape[2] // tm
    row = lambda i: (i, 0)
    full = lambda i: (0, 0)
    return pl.pallas_call(
        _merge_kernel,
        grid=(t // tm,),
        in_specs=[pl.BlockSpec((tm, d), row),
                  pl.BlockSpec((1, ya.shape[1], tm),
                               lambda i: (i // tiles_per_seq, 0, i % tiles_per_seq)),
                  pl.BlockSpec((tm, ys.shape[1]), row),
                  pl.BlockSpec((tm, gp.shape[1]), row),
                  pl.BlockSpec(gb.shape, full),
                  pl.BlockSpec(wa.shape, full),
                  pl.BlockSpec(ws.shape, full),
                  pl.BlockSpec(wo.shape, full)],
        out_specs=pl.BlockSpec((tm, d), row),
        out_shape=jax.ShapeDtypeStruct((t, d), F32),
        compiler_params=_cparams("parallel"),
        name="merge",
    )(x, ya, ys, gp, gb, wa, ws, wo)


def _ffn_kernel(res_ref, halo_ref, nw_ref, wup_ref, cw_ref, cb_ref, wd_ref, fnw_ref, o_ref, h_scr,
                u_scr, *, tm, hid, tiles_per_seq, final_norm):
    i = pl.program_id(0)
    nw = nw_ref[...]
    hh = _rmsnorm(halo_ref[...], nw)
    hh = jnp.where(i % tiles_per_seq == 0, 0.0, hh)
    h_scr[0:HALO, :] = hh.astype(BF16)
    h_scr[HALO:, :] = _rmsnorm(res_ref[...], nw).astype(BF16)
    th = hid // FFN_SPLIT
    rows = tm // FFN_ROW_PARTS

    for part in range(FFN_ROW_PARTS):
        r0 = part * rows
        hx = h_scr[r0:r0 + HALO + rows, :]

        def branch(c0, slot):
            u_scr[slot] = jnp.dot(hx, wup_ref[:, c0:c0 + th], preferred_element_type=F32)
            cw = cw_ref[:, c0:c0 + th]
            out = cb_ref[:, c0:c0 + th]
            for kk in range(FFN_CONV):
                off = HALO - (FFN_CONV - 1) + kk
                out = out + cw[kk:kk + 1] * u_scr[slot, off:off + rows, :]
            return out

        total = res_ref[r0:r0 + rows, :]
        for c in range(FFN_SPLIT):
            slot = 2 * (part * FFN_SPLIT + c)
            act = (_silu(branch(hid + c * th, slot)) * branch(c * th, slot + 1)).astype(BF16)
            total = total + jnp.dot(act, wd_ref[c * th:(c + 1) * th, :],
                                    preferred_element_type=F32)
        o_ref[r0:r0 + rows, :] = _rmsnorm(total, fnw_ref[...]) if final_norm else total


def _ffn(res, nw, w_up, cw, cb, w_down, fnw, seq, tm, final_norm):
    t, d = res.shape
    hid = w_down.shape[0]
    row = lambda i: (i, 0)
    full = lambda i: (0, 0)
    once = pl.Buffered(1)
    halo_blocks = tm // HALO
    return pl.pallas_call(
        functools.partial(_ffn_kernel, tm=tm, hid=hid, tiles_per_seq=seq // tm,
                          final_norm=final_norm),
        grid=(t // tm,),
        in_specs=[pl.BlockSpec((tm, d), row),
                  pl.BlockSpec((HALO, d), lambda i: (jnp.maximum(i * halo_blocks - 1, 0), 0)),
                  pl.BlockSpec((1, d), full),
                  pl.BlockSpec(w_up.shape, full, pipeline_mode=once),
                  pl.BlockSpec(cw.shape, full, pipeline_mode=once),
                  pl.BlockSpec(cb.shape, full, pipeline_mode=once),
                  pl.BlockSpec(w_down.shape, full, pipeline_mode=once),
                  pl.BlockSpec((1, d), full)],
        out_specs=pl.BlockSpec((tm, d), row),
        out_shape=jax.ShapeDtypeStruct((t, d), F32),
        scratch_shapes=[pltpu.VMEM((HALO + tm, d), BF16),
                        pltpu.VMEM((2 * FFN_SPLIT * FFN_ROW_PARTS, HALO + tm // FFN_ROW_PARTS,
                                    hid // FFN_SPLIT), F32)],
        compiler_params=_cparams("parallel"),
        name="ffn",
    )(res, res, nw, w_up, cw, cb, w_down, fnw)


def kernel(x, mix_norm_w, w_in, gate_b, lambda_q1, lambda_k1, lambda_q2, lambda_k2, attn_subln_w,
           ssd_conv_w, ssd_conv_b, ssd_dt_bias, ssd_a_log, ssd_d, ssd_norm_w, w_attn_branch,
           w_ssd_branch, w_out, ffn_norm_w, w_up, ffn_conv_w, ffn_conv_b, w_down, final_norm_w):
    bsz, seq, d = x.shape
    t = bsz * seq
    depth = w_in.shape[0]
    att_w = ATT_HEADS * ATT_V_DIM
    inner = SSD_HEADS * SSD_HEAD_DIM
    cdim = inner + 2 * SSD_GROUPS * SSD_STATE
    c_q, c_k, c_v, c_z, c_x, c_dt = (att_w, 2 * att_w, 3 * att_w, 3 * att_w + inner,
                                     3 * att_w + inner + cdim, 3 * att_w + inner + cdim + SSD_HEADS)
    tk = min(ATT_KEYS, seq // 2)
    lc = min(SSD_TILE, seq)
    tm = min(ROW_TILE, seq)
    mm_rows = min(MM_ROWS, t)
    slopes = (jnp.exp2(-8.0 * jnp.arange(1, ATT_HEADS + 1, dtype=F32) / ATT_HEADS) * LOG2E).astype(F32)
    pad_heads = lambda v: jnp.pad(v.astype(F32), (0, LANES - SSD_HEADS)).reshape(1, LANES)

    res = x.reshape(t, d)
    for l in range(depth):
        lam_init = 0.8 - 0.6 * math.exp(-0.3 * l)
        wt = jnp.swapaxes(w_in[l], 0, 1).astype(BF16)
        wt_gate = wt[c_dt:]
        wdt = jnp.pad(wt[c_x:c_dt], ((0, LANES - SSD_HEADS), (0, 0)))

        h, dt_raw, k = _norm_dt(res, mix_norm_w[l].reshape(1, d), wdt, wt, c_q // att_w, att_w,
                                mm_rows)
        blocks = lambda c0, c1: list(range(c0 // att_w, c1 // att_w))
        z, xbc, gate_pre = _matmul_nt(
            h, [(wt, blocks(c_v, c_z)), (wt, blocks(c_z, c_x)),
                (wt_gate, blocks(0, wt_gate.shape[0]))], att_w, tm, "proj_ssd")
        qvt = _proj_t(wt, 0, c_k // att_w, att_w, ATT_HEAD_DIM ** -0.5 * LOG2E, h, bsz, seq, tk,
                      min(MM_ROWS, seq))

        lamv = jnp.stack([lambda_q1[l], lambda_k1[l], lambda_q2[l], lambda_k2[l]]).astype(F32)
        y_att = _attention(slopes, lamv, attn_subln_w[l].reshape(ATT_V_DIM, 1).astype(F32), qvt,
                           k.reshape(bsz, seq, att_w), bsz, seq, tk, lam_init)
        y_ssd = _ssd(z.reshape(bsz, seq, inner), xbc.reshape(bsz, seq, cdim),
                     dt_raw.reshape(bsz, seq, LANES), ssd_conv_w[l].astype(F32),
                     ssd_conv_b[l].reshape(1, cdim).astype(F32), pad_heads(ssd_dt_bias[l]),
                     pad_heads(ssd_a_log[l]),
                     jnp.repeat(ssd_d[l].astype(F32), SSD_HEAD_DIM).reshape(1, inner),
                     ssd_norm_w[l].reshape(1, inner).astype(F32), lc)
        res = _merge(res, y_att, y_ssd.reshape(t, inner), gate_pre,
                     gate_b[l].reshape(1, -1).astype(F32), w_attn_branch[l].astype(BF16),
                     w_ssd_branch[l].astype(BF16), w_out[l].astype(BF16), tm)
        res = _ffn(res, ffn_norm_w[l].reshape(1, d).astype(F32), w_up[l].astype(BF16),
                   ffn_conv_w[l].astype(F32), ffn_conv_b[l].reshape(1, -1).astype(F32),
                   w_down[l].astype(BF16), final_norm_w.reshape(1, d).astype(F32), seq, tm,
                   l == depth - 1)
    return res.reshape(bsz, seq, d)
```
